```python
import math
import jax, jax.numpy as jnp
from jax import lax
import numpy as np

D_MODEL = 2048
BATCH = 2
SEQ = 16384
DEPTH = 1

CHUNK = 64
N_META = 16
HEAD_DIM = 256
N_HEADS_FOX = D_MODEL // (2 * HEAD_DIM)
N_HEADS_SB = D_MODEL // (2 * HEAD_DIM)
D_FOX = N_HEADS_FOX * HEAD_DIM
D_SB = N_HEADS_SB * HEAD_DIM
D_MIX = D_FOX + D_SB
IN_COLS = 3 * D_FOX + N_HEADS_FOX + 3 * D_SB
D_FF = 256 * ((8 * D_MODEL // 3 + 255) // 256)
CONV_WIDTH = 3
Q_BLOCK = 128
LN_EPS = 1e-5
RMS_EPS = 1e-6
FOX_GATE_BIAS = 2.0
DEEPNORM_ALPHA = (2.0 * DEPTH) ** 0.25
DEEPNORM_BETA = (8.0 * DEPTH) ** -0.25

kernel_name = "hymba_fox_stickbreaking_deepnorm_convffn"


def layer_norm(x, g, b):
    xf = x.astype(jnp.float32)
    mu = jnp.mean(xf, axis=-1, keepdims=True)
    var = jnp.mean(jnp.square(xf - mu), axis=-1, keepdims=True)
    y = (xf - mu) * lax.rsqrt(var + LN_EPS)
    return (y * g.astype(jnp.float32) + b.astype(jnp.float32)).astype(x.dtype)


def rms_norm(x, g):
    xf = x.astype(jnp.float32)
    y = xf * lax.rsqrt(jnp.mean(jnp.square(xf), axis=-1, keepdims=True) + RMS_EPS)
    return (y * g.astype(jnp.float32)).astype(x.dtype)


def _heads(t, n_heads, lp):
    b, l, _ = t.shape
    t = t.reshape(b, l, n_heads, HEAD_DIM).transpose(0, 2, 1, 3)
    return jnp.pad(t, ((0, 0), (0, 0), (0, lp - l), (0, 0)))


def _merge_heads(o, l):
    b, h, lp, dh = o.shape
    return o.transpose(0, 2, 1, 3).reshape(b, lp, h * dh)[:, :l]


def forgetting_attention(q, k, v, log_f):
    b, h, lp, dh = q.shape
    scale = 1.0 / math.sqrt(dh)
    c = jnp.cumsum(log_f, axis=-1)
    outs = []
    for i in range(lp // Q_BLOCK):
        lo, n = i * Q_BLOCK, (i + 1) * Q_BLOCK
        s = jnp.einsum('bhqd,bhkd->bhqk', q[:, :, lo:n], k[:, :, :n],
                       preferred_element_type=jnp.float32) * scale
        s = s + c[:, :, lo:n, None] - c[:, :, None, :n]
        mask = jnp.arange(n)[None, :] <= (lo + jnp.arange(Q_BLOCK))[:, None]
        p = jax.nn.softmax(jnp.where(mask, s, -jnp.inf), axis=-1)
        outs.append(jnp.einsum('bhqk,bhkd->bhqd', p.astype(v.dtype), v[:, :, :n]))
    return jnp.concatenate(outs, axis=2)


def stick_breaking_attention(q, k, v):
    b, h, lp, dh = q.shape
    scale = 1.0 / math.sqrt(dh)
    hi = lax.Precision.HIGHEST
    tri_in = jnp.tri(Q_BLOCK, k=-1, dtype=jnp.float32)
    outs = []
    for i in range(lp // Q_BLOCK):
        lo, n, nb = i * Q_BLOCK, (i + 1) * Q_BLOCK, i + 1
        z = jnp.einsum('bhqd,bhkd->bhqk', q[:, :, lo:n], k[:, :, :n],
                       preferred_element_type=jnp.float32) * scale
        strict = jnp.arange(n)[None, :] < (lo + jnp.arange(Q_BLOCK))[:, None]
        log_beta = jax.nn.log_sigmoid(z)
        log_1m = jnp.where(strict, log_beta - z, 0.0)
        xb = log_1m.reshape(b, h, Q_BLOCK, nb, Q_BLOCK)
        rest_in = jnp.einsum('bhqkc,ce->bhqke', xb, tri_in, precision=hi)
        after = jnp.einsum('bhqk,kj->bhqj', xb.sum(-1),
                           jnp.tri(nb, k=-1, dtype=jnp.float32), precision=hi)
        rest = (rest_in + after[..., None]).reshape(b, h, Q_BLOCK, n)
        a = jnp.where(strict, jnp.exp(log_beta + rest), 0.0)
        outs.append(jnp.einsum('bhqk,bhkd->bhqd', a.astype(v.dtype), v[:, :, :n]))
    return jnp.concatenate(outs, axis=2)


def conv_ffn(h, w_gu, conv_w, conv_b, w_down):
    gu = h @ w_gu
    g, u = gu[..., :D_FF], gu[..., D_FF:]
    g = lax.conv_general_dilated(
        g, conv_w[:, None, :].astype(g.dtype), window_strides=(1,),
        padding=[(CONV_WIDTH - 1, 0)], dimension_numbers=('NWC', 'WIO', 'NWC'),
        feature_group_count=D_FF) + conv_b.astype(g.dtype)
    return (jax.nn.silu(g) * u) @ w_down


def setup_inputs(seed: int = 0) -> dict:
    key = jax.random.key(seed)
    ks = jax.random.split(key, 20)
    f32 = jnp.float32
    nrm = lambda k, shp: jax.random.normal(k, shp, f32)
    x = nrm(ks[0], (BATCH, SEQ, D_MODEL))
    meta = nrm(ks[1], (N_META, D_MODEL))
    ln0_g = 1.0 + 0.02 * nrm(ks[2], (D_MODEL,))
    ln0_b = 0.02 * nrm(ks[3], (D_MODEL,))
    w_in = nrm(ks[4], (DEPTH, D_MODEL, IN_COLS)) * D_MODEL ** -0.5
    b_f = FOX_GATE_BIAS + 0.1 * nrm(ks[5], (DEPTH, N_HEADS_FOX))
    g_fox = 1.0 + 0.02 * nrm(ks[6], (DEPTH, D_FOX))
    g_sb = 1.0 + 0.02 * nrm(ks[7], (DEPTH, D_SB))
    w_o = nrm(ks[8], (DEPTH, D_MIX, D_MODEL)) * (D_MIX ** -0.5) * DEEPNORM_BETA
    ln1_g = 1.0 + 0.02 * nrm(ks[9], (DEPTH, D_MODEL))
    ln1_b = 0.02 * nrm(ks[10], (DEPTH, D_MODEL))
    w_gu = nrm(ks[11], (DEPTH, D_MODEL, 2 * D_FF)) * D_MODEL ** -0.5
    conv_w = nrm(ks[12], (DEPTH, CONV_WIDTH, D_FF)) * CONV_WIDTH ** -0.5
    conv_b = 0.01 * nrm(ks[13], (DEPTH, D_FF))
    w_down = nrm(ks[14], (DEPTH, D_FF, D_MODEL)) * (D_FF ** -0.5) * DEEPNORM_BETA
    ln2_g = 1.0 + 0.02 * nrm(ks[15], (DEPTH, D_MODEL))
    ln2_b = 0.02 * nrm(ks[16], (DEPTH, D_MODEL))
    return {"x": x, "meta": meta, "ln0_g": ln0_g, "ln0_b": ln0_b, "w_in": w_in,
            "b_f": b_f, "g_fox": g_fox, "g_sb": g_sb, "w_o": w_o,
            "ln1_g": ln1_g, "ln1_b": ln1_b, "w_gu": w_gu, "conv_w": conv_w,
            "conv_b": conv_b, "w_down": w_down, "ln2_g": ln2_g, "ln2_b": ln2_b}


def reference(x, meta, ln0_g, ln0_b, w_in, b_f, g_fox, g_sb, w_o, ln1_g, ln1_b,
              w_gu, conv_w, conv_b, w_down, ln2_g, ln2_b):
    b, s, d = x.shape
    h = jnp.concatenate([jnp.broadcast_to(meta[None].astype(x.dtype), (b, N_META, d)), x], axis=1)
    h = layer_norm(h, ln0_g, ln0_b)
    l = h.shape[1]
    lp = -(-l // Q_BLOCK) * Q_BLOCK
    cuts = [D_FOX, 2 * D_FOX, 3 * D_FOX, 3 * D_FOX + N_HEADS_FOX,
            3 * D_FOX + N_HEADS_FOX + D_SB, 3 * D_FOX + N_HEADS_FOX + 2 * D_SB]
    for layer in range(DEPTH):
        proj = h @ w_in[layer]
        q_f, k_f, v_f, f_logit, q_s, k_s, v_s = jnp.split(proj, cuts, axis=-1)
        log_f = jax.nn.log_sigmoid((f_logit + b_f[layer]).astype(jnp.float32))
        log_f = jnp.pad(log_f.transpose(0, 2, 1), ((0, 0), (0, 0), (0, lp - l)))
        o_f = forgetting_attention(_heads(q_f, N_HEADS_FOX, lp), _heads(k_f, N_HEADS_FOX, lp),
                                   _heads(v_f, N_HEADS_FOX, lp), log_f)
        o_s = stick_breaking_attention(_heads(q_s, N_HEADS_SB, lp), _heads(k_s, N_HEADS_SB, lp),
                                       _heads(v_s, N_HEADS_SB, lp))
        mix = jnp.concatenate([rms_norm(_merge_heads(o_f, l), g_fox[layer]),
                               rms_norm(_merge_heads(o_s, l), g_sb[layer])], axis=-1)
        h = layer_norm(DEEPNORM_ALPHA * h + mix @ w_o[layer], ln1_g[layer], ln1_b[layer])
        ffn = conv_ffn(h, w_gu[layer], conv_w[layer], conv_b[layer], w_down[layer])
        h = layer_norm(DEEPNORM_ALPHA * h + ffn, ln2_g[layer], ln2_b[layer])
    return h[:, N_META:]
```

```python
import functools
import math

import jax
import jax.numpy as jnp
from jax import lax
from jax.experimental import pallas as pl
from jax.experimental.pallas import tpu as pltpu

N_META = 16
HEAD_DIM = 256
N_HEADS = 4
D_GROUP = N_HEADS * HEAD_DIM
CONV_WIDTH = 3
LN_EPS = 1e-5
RMS_EPS = 1e-6
DEEPNORM_ALPHA = 2.0 ** 0.25
GATE_COLS = 128
META_ROWS = 128
NEG_BIG = -1e30

F32 = jnp.float32
BF16 = jnp.bfloat16

VMEM_LIMIT = 56 * 1024 * 1024


def _layer_norm(x, g, b):
    mu = jnp.mean(x, axis=-1, keepdims=True)
    xc = x - mu
    var = jnp.mean(xc * xc, axis=-1, keepdims=True)
    return xc * lax.rsqrt(var + LN_EPS) * g + b


def _rms_norm(x, g):
    ms = jnp.mean(x * x, axis=-1, keepdims=True)
    return x * lax.rsqrt(ms + RMS_EPS) * g


def _log_sigmoid(z):
    return jnp.minimum(z, 0.0) - jnp.log1p(jnp.exp(-jnp.abs(z)))


def _split3(x):
    hi = x.astype(BF16)
    r1 = x - hi.astype(F32)
    mid = r1.astype(BF16)
    lo = (r1 - mid.astype(F32)).astype(BF16)
    return hi, mid, lo


def _resident(shape):
    nd = len(shape)
    return pl.BlockSpec(shape, lambda *_: (0,) * nd, pipeline_mode=pl.Buffered(1))


def _in_proj_kernel(x_ref, g_ref, b_ref, w_ref, bf_ref, tril_ref,
                    qf_ref, kf_ref, vf_ref, qs_ref, ks_ref, vs_ref, nb_ref,
                    carry_ref, *, tiles_per_seq):
    i = pl.program_id(0)
    scale = 1.0 / math.sqrt(HEAD_DIM)
    hn = _layer_norm(x_ref[...], g_ref[...], b_ref[...])
    hb = hn.astype(BF16)
    outs = (qf_ref, kf_ref, vf_ref, qs_ref, ks_ref, vs_ref)
    for n, o_ref in enumerate(outs):
        y = jnp.dot(hb, w_ref[:, n * D_GROUP:(n + 1) * D_GROUP],
                    preferred_element_type=F32)
        if n in (0, 3):
            y = y * scale
        o_ref[...] = y.astype(BF16)

    f_logit = jnp.dot(hb, w_ref[:, 6 * D_GROUP:], preferred_element_type=F32)
    log_f = _log_sigmoid(f_logit + bf_ref[...])

    @pl.when(i % tiles_per_seq == 0)
    def _():
        carry_ref[...] = jnp.zeros_like(carry_ref)

    tril = tril_ref[...]
    hi, mid, lo = _split3(log_f)
    c = (jnp.dot(tril, hi, preferred_element_type=F32)
         + jnp.dot(tril, mid, preferred_element_type=F32)
         + jnp.dot(tril, lo, preferred_element_type=F32))
    c = c + carry_ref[0:1, :]
    carry_ref[...] = jnp.broadcast_to(c[-1:, :], carry_ref.shape)
    nb_ref[...] = -jnp.transpose(c)[0:8, :]


def _in_proj(x2d, ln_g, ln_b, w_all, bf_row, *, rows_per_seq, tm):
    rows, d = x2d.shape
    n_tiles = rows // tm
    tril = jnp.tril(jnp.ones((tm, tm), BF16))
    qkv_shape = jax.ShapeDtypeStruct((rows, D_GROUP), BF16)
    qkv_spec = pl.BlockSpec((tm, D_GROUP), lambda i: (i, 0))
    return pl.pallas_call(
        functools.partial(_in_proj_kernel, tiles_per_seq=rows_per_seq // tm),
        grid=(n_tiles,),
        in_specs=[
            pl.BlockSpec((tm, d), lambda i: (i, 0)),
            _resident((1, d)), _resident((1, d)),
            _resident(w_all.shape),
            _resident((1, GATE_COLS)),
            _resident((tm, tm)),
        ],
        out_specs=[qkv_spec] * 6 + [pl.BlockSpec((8, tm), lambda i: (0, i))],
        out_shape=[qkv_shape] * 6 + [jax.ShapeDtypeStruct((8, rows), F32)],
        scratch_shapes=[pltpu.VMEM((8, GATE_COLS), F32)],
        compiler_params=pltpu.CompilerParams(
            dimension_semantics=("arbitrary",), vmem_limit_bytes=VMEM_LIMIT),
        name="in_proj",
    )(x2d, ln_g, ln_b, w_all, bf_row, tril)


def _tile_ids(tq, tk):
    row = lax.broadcasted_iota(jnp.int32, (tq, tk), 0)
    col = lax.broadcasted_iota(jnp.int32, (tq, tk), 1)
    return row, col


def _qk(q, k):
    return lax.dot_general(q, k, (((1,), (1,)), ((), ())), preferred_element_type=F32)


def _fox_kernel(*refs, t, has_prefix):
    if has_prefix:
        q_ref, k_ref, v_ref, nb_ref, kp_ref, vp_ref, pb_ref, o_ref = refs
    else:
        q_ref, k_ref, v_ref, nb_ref, o_ref = refs
    h = pl.program_id(1)
    i = pl.program_id(2)
    q = q_ref[0]

    def update(carry, a, v):
        m, l, acc = carry
        m_new = jnp.maximum(m, jnp.max(a, axis=-1, keepdims=True))
        alpha = jnp.exp(m - m_new)
        p = jnp.exp(a - m_new)
        l = alpha * l + jnp.sum(p, axis=-1, keepdims=True)
        acc = alpha * acc + jnp.dot(p.astype(BF16), v, preferred_element_type=F32)
        return m_new, l, acc

    def block(j, carry, diagonal):
        start = pl.multiple_of(j * t, t)
        a = _qk(q, k_ref[0, pl.ds(start, t), :]) + nb_ref[0, 0, pl.ds(j, 1), :]
        if diagonal:
            row, col = _tile_ids(t, t)
            a = jnp.where(col <= row, a, NEG_BIG)
        return update(carry, a, v_ref[0, pl.ds(start, t), :])

    carry = (jnp.full((t, 1), NEG_BIG, F32), jnp.zeros((t, 1), F32),
             jnp.zeros((t, HEAD_DIM), F32))
    carry = block(i, carry, True)
    carry = lax.fori_loop(0, i, lambda n, c: block(i - 1 - n, c, False), carry)
    if has_prefix:
        a = _qk(q, kp_ref[...]) + pb_ref[pl.ds(h, 1), :]
        carry = update(carry, a, vp_ref[...])
    _, l, acc = carry
    o_ref[0] = acc / l


def _sb_kernel(*refs, t, has_prefix):
    if has_prefix:
        q_ref, k_ref, v_ref, tri_ref, kp_ref, vp_ref, trip_ref, o_ref = refs
    else:
        q_ref, k_ref, v_ref, tri_ref, o_ref = refs
    i = pl.program_id(2)
    q = q_ref[0]

    def update(carry, z, v, tri, valid):
        run, acc = carry
        log_beta = _log_sigmoid(z)
        log_1m = log_beta - z
        if valid is not None:
            log_1m = jnp.where(valid, log_1m, 0.0)
        hi = log_1m.astype(BF16)
        lo = (log_1m - hi.astype(F32)).astype(BF16)
        rest = (jnp.dot(hi, tri, preferred_element_type=F32)
                + jnp.dot(lo, tri, preferred_element_type=F32))
        a = jnp.exp(log_beta + rest + run)
        if valid is not None:
            a = jnp.where(valid, a, 0.0)
        acc = acc + jnp.dot(a.astype(BF16), v, preferred_element_type=F32)
        run = run + jnp.sum(log_1m, axis=-1, keepdims=True)
        return run, acc

    def block(j, carry, diagonal):
        start = pl.multiple_of(j * t, t)
        z = _qk(q, k_ref[0, pl.ds(start, t), :])
        valid = None
        if diagonal:
            row, col = _tile_ids(t, t)
            valid = col < row
        return update(carry, z, v_ref[0, pl.ds(start, t), :], tri_ref[...], valid)

    carry = (jnp.zeros((t, 1), F32), jnp.zeros((t, HEAD_DIM), F32))
    carry = block(i, carry, True)
    carry = lax.fori_loop(0, i, lambda n, c: block(i - 1 - n, c, False), carry)
    if has_prefix:
        _, col = _tile_ids(t, META_ROWS)
        carry = update(carry, _qk(q, kp_ref[...]), vp_ref[...], trip_ref[...],
                       col < N_META)
    o_ref[0] = carry[1]


def _strict_lower(n):
    return jnp.tril(jnp.ones((n, n), BF16), k=-1)


def _attention_specs(batch, seq, t):
    q_spec = pl.BlockSpec((1, t, HEAD_DIM), lambda b, h, i: (b, i, h))
    kv_spec = pl.BlockSpec((1, seq, HEAD_DIM), lambda b, h, i: (b, 0, h))
    o_spec = pl.BlockSpec((1, t, HEAD_DIM), lambda b, h, i: (b, i, h))
    out_shape = jax.ShapeDtypeStruct((batch, seq, D_GROUP), F32)
    params = pltpu.CompilerParams(
        dimension_semantics=("arbitrary", "arbitrary", "arbitrary"),
        vmem_limit_bytes=VMEM_LIMIT)
    return q_spec, kv_spec, o_spec, out_shape, params


def _prefix_spec():
    return pl.BlockSpec((META_ROWS, HEAD_DIM), lambda b, h, i: (0, h))


def _fox_attention(q, k, v, nb, prefix, *, t):
    batch, seq, _ = q.shape
    q_spec, kv_spec, o_spec, out_shape, params = _attention_specs(batch, seq, t)
    nb4 = nb.reshape(8, batch, seq // t, t)
    in_specs = [q_spec, kv_spec, kv_spec,
                pl.BlockSpec((1, 1, seq // t, t), lambda b, h, i: (h, b, 0, 0))]
    args = [q, k, v, nb4]
    if prefix is not None:
        in_specs += [_prefix_spec(), _prefix_spec(), _resident((8, META_ROWS))]
        args += list(prefix)
    return pl.pallas_call(
        functools.partial(_fox_kernel, t=t, has_prefix=prefix is not None),
        grid=(batch, N_HEADS, seq // t),
        in_specs=in_specs, out_specs=o_spec, out_shape=out_shape,
        compiler_params=params, name="fox_attention",
    )(*args)


def _sb_attention(q, k, v, prefix, *, t):
    batch, seq, _ = q.shape
    q_spec, kv_spec, o_spec, out_shape, params = _attention_specs(batch, seq, t)
    in_specs = [q_spec, kv_spec, kv_spec, _resident((t, t))]
    args = [q, k, v, _strict_lower(t)]
    if prefix is not None:
        in_specs += [_prefix_spec(), _prefix_spec(), _resident((META_ROWS, META_ROWS))]
        args += list(prefix) + [_strict_lower(META_ROWS)]
    return pl.pallas_call(
        functools.partial(_sb_kernel, t=t, has_prefix=prefix is not None),
        grid=(batch, N_HEADS, seq // t),
        in_specs=in_specs, out_specs=o_spec, out_shape=out_shape,
        compiler_params=params, name="sb_attention",
    )(*args)


def _out_proj_kernel(of_ref, os_ref, x_ref, g0_ref, b0_ref, gf_ref, gs_ref, wo_ref,
                     g1_ref, b1_ref, h_ref):
    mf = _rms_norm(of_ref[...], gf_ref[...]).astype(BF16)
    ms = _rms_norm(os_ref[...], gs_ref[...]).astype(BF16)
    y = (jnp.dot(mf, wo_ref[0:D_GROUP, :], preferred_element_type=F32)
         + jnp.dot(ms, wo_ref[D_GROUP:, :], preferred_element_type=F32))
    h0 = _layer_norm(x_ref[...], g0_ref[...], b0_ref[...])
    h_ref[...] = _layer_norm(DEEPNORM_ALPHA * h0 + y, g1_ref[...], b1_ref[...])


def _out_proj(o_f, o_s, x2d, ln0_g, ln0_b, g_fox, g_sb, w_o, ln1_g, ln1_b, *, tm):
    rows, d = x2d.shape
    row_spec = lambda width: pl.BlockSpec((tm, width), lambda i: (i, 0))
    return pl.pallas_call(
        _out_proj_kernel,
        grid=(rows // tm,),
        in_specs=[row_spec(D_GROUP), row_spec(D_GROUP), row_spec(d),
                  _resident((1, d)), _resident((1, d)),
                  _resident((1, D_GROUP)), _resident((1, D_GROUP)),
                  _resident(w_o.shape), _resident((1, d)), _resident((1, d))],
        out_specs=row_spec(d),
        out_shape=jax.ShapeDtypeStruct((rows, d), F32),
        compiler_params=pltpu.CompilerParams(
            dimension_semantics=("arbitrary",), vmem_limit_bytes=VMEM_LIMIT),
        name="out_proj",
    )(o_f, o_s, x2d, ln0_g, ln0_b, g_fox, g_sb, w_o, ln1_g, ln1_b)


def _gate_kernel(h_ref, w_ref, g_ref):
    g_ref[...] = jnp.dot(h_ref[...].astype(BF16), w_ref[...], preferred_element_type=F32)


def _gate_rows(h_rows, w_gate, *, tf):
    rows, d = h_rows.shape
    d_ff = w_gate.shape[1]
    return pl.pallas_call(
        _gate_kernel,
        grid=(d_ff // tf,),
        in_specs=[_resident((rows, d)), pl.BlockSpec((d, tf), lambda f: (0, f))],
        out_specs=pl.BlockSpec((rows, tf), lambda f: (0, f)),
        out_shape=jax.ShapeDtypeStruct((rows, d_ff), F32),
        compiler_params=pltpu.CompilerParams(dimension_semantics=("arbitrary",)),
        name="gate_rows",
    )(h_rows, w_gate)


def _ffn_kernel(h_ref, wg_ref, wu_ref, wd_ref, cw_ref, cb_ref, gi_ref, g2_ref, b2_ref,
                o_ref, hb_ref, acc_ref, hist_ref, *, tm):
    i = pl.program_id(1)
    f = pl.program_id(2)

    @pl.when(f == 0)
    def _():
        hb_ref[...] = h_ref[0].astype(BF16)
        acc_ref[...] = jnp.zeros_like(acc_ref)

    @pl.when(i == 0)
    def _():
        hist_ref[f] = gi_ref[...]

    hb = hb_ref[...]
    g = jnp.dot(hb, wg_ref[...], preferred_element_type=F32)
    u = jnp.dot(hb, wu_ref[...], preferred_element_type=F32)
    hist = hist_ref[f]
    prev1 = hist[7:8, :]
    prev2 = hist[6:7, :]
    hist_ref[f] = g[tm - 8:, :]
    row = lax.broadcasted_iota(jnp.int32, g.shape, 0)
    g1 = jnp.where(row == 0, prev1, pltpu.roll(g, 1, 0))
    g2 = jnp.where(row == 0, prev2, jnp.where(row == 1, prev1, pltpu.roll(g, 2, 0)))
    cw = cw_ref[...]
    gc = cw[0:1, :] * g2 + cw[1:2, :] * g1 + cw[2:3, :] * g + cb_ref[...]
    act = gc * (1.0 / (1.0 + jnp.exp(-gc))) * u
    acc_ref[...] += jnp.dot(act.astype(BF16), wd_ref[...], preferred_element_type=F32)

    @pl.when(f == pl.num_programs(2) - 1)
    def _():
        o_ref[0] = _layer_norm(DEEPNORM_ALPHA * h_ref[0] + acc_ref[...],
                               g2_ref[...], b2_ref[...])


def _ffn(h1, w_gu, w_down, conv_w, conv_b, g_init, ln2_g, ln2_b, *, tm, tf):
    batch, seq, d = h1.shape
    d_ff = w_down.shape[0]
    n_f = d_ff // tf
    return pl.pallas_call(
        functools.partial(_ffn_kernel, tm=tm),
        grid=(batch, seq // tm, n_f),
        in_specs=[
            pl.BlockSpec((1, tm, d), lambda b, i, f: (b, i, 0)),
            pl.BlockSpec((d, tf), lambda b, i, f: (0, f)),
            pl.BlockSpec((d, tf), lambda b, i, f: (0, n_f + f)),
            pl.BlockSpec((tf, d), lambda b, i, f: (f, 0)),
            pl.BlockSpec((CONV_WIDTH, tf), lambda b, i, f: (0, f)),
            pl.BlockSpec((1, tf), lambda b, i, f: (0, f)),
            pl.BlockSpec((8, tf), lambda b, i, f: (0, f)),
            _resident((1, d)), _resident((1, d)),
        ],
        out_specs=pl.BlockSpec((1, tm, d), lambda b, i, f: (b, i, 0)),
        out_shape=jax.ShapeDtypeStruct((batch, seq, d), F32),
        scratch_shapes=[pltpu.VMEM((tm, d), BF16), pltpu.VMEM((tm, d), F32),
                        pltpu.VMEM((n_f, 8, tf), F32)],
        compiler_params=pltpu.CompilerParams(
            dimension_semantics=("arbitrary", "arbitrary", "arbitrary"),
            vmem_limit_bytes=VMEM_LIMIT),
        name="conv_ffn",
    )(h1, w_gu, w_gu, w_down, conv_w, conv_b, g_init, ln2_g, ln2_b)


def _pick(n, preferred):
    return preferred if n % preferred == 0 else n


def kernel(x, meta, ln0_g, ln0_b, w_in, b_f, g_fox, g_sb, w_o, ln1_g, ln1_b, w_gu, conv_w,
           conv_b, w_down, ln2_g, ln2_b):
    batch, seq, d = x.shape
    assert w_in.shape[0] == 1, "single layer"
    assert d == 2 * D_GROUP
    rows = batch * seq
    d_ff = w_down.shape[1]
    tm = _pick(seq, 512)
    t_att = _pick(seq, 256)
    tm_ffn = _pick(seq, 512)
    tf = _pick(d_ff, 512)

    row = lambda a: a.reshape(1, -1).astype(F32)
    w = w_in[0]
    gate_w = jnp.pad(w[:, 3 * D_GROUP:3 * D_GROUP + N_HEADS], ((0, 0), (0, GATE_COLS - N_HEADS)))
    w_all = jnp.concatenate([w[:, :3 * D_GROUP], w[:, 3 * D_GROUP + N_HEADS:], gate_w],
                            axis=1).astype(BF16)
    bf_row = jnp.pad(b_f[0].astype(F32), (0, GATE_COLS - N_HEADS)).reshape(1, GATE_COLS)
    w_o_b = w_o[0].astype(BF16)
    w_gu_b = w_gu[0].astype(BF16)
    w_down_b = w_down[0].astype(BF16)
    ln0 = (row(ln0_g), row(ln0_b))
    ln1 = (row(ln1_g[0]), row(ln1_b[0]))
    ln2 = (row(ln2_g[0]), row(ln2_b[0]))
    gf, gs = row(g_fox[0]), row(g_sb[0])

    def mixer(x2d, seqs, seq_len, tm_rows, t, prefix_f, prefix_s):
        qf, kf, vf, qs, ks, vs, nb = _in_proj(x2d, *ln0, w_all, bf_row,
                                              rows_per_seq=seq_len, tm=tm_rows)
        shape3 = (seqs, seq_len, D_GROUP)
        o_f = _fox_attention(qf.reshape(shape3), kf.reshape(shape3), vf.reshape(shape3),
                             nb, prefix_f(kf, vf, nb) if prefix_f else None, t=t)
        o_s = _sb_attention(qs.reshape(shape3), ks.reshape(shape3), vs.reshape(shape3),
                            prefix_s(ks, vs) if prefix_s else None, t=t)
        h1 = _out_proj(o_f.reshape(-1, D_GROUP), o_s.reshape(-1, D_GROUP), x2d, *ln0,
                       gf, gs, w_o_b, *ln1, tm=tm_rows)
        return h1, (kf, vf, nb, ks, vs)

    meta_pad = jnp.zeros((META_ROWS, d), F32).at[:N_META].set(meta.astype(F32))
    h1_meta, (kf_m, vf_m, nb_m, ks_m, vs_m) = mixer(meta_pad, 1, META_ROWS, META_ROWS,
                                                    META_ROWS, None, None)
    valid = jnp.arange(META_ROWS)[None, :] < N_META
    pbias = jnp.where(valid, nb_m - nb_m[:, N_META - 1:N_META], NEG_BIG)
    g_meta = _gate_rows(h1_meta[:N_META], w_gu_b[:, :d_ff], tf=tf)
    g_init = g_meta[N_META - 8:N_META]

    h1, _ = mixer(x.reshape(rows, d).astype(F32), batch, seq, tm, t_att,
                  lambda *_: (kf_m, vf_m, pbias), lambda *_: (ks_m, vs_m))
    out = _ffn(h1.reshape(batch, seq, d), w_gu_b, w_down_b, conv_w[0].astype(F32),
               row(conv_b[0]), g_init, *ln2, tm=tm_ffn, tf=tf)
    return out.astype(x.dtype)
```

```python
import functools
import math

import jax
import jax.numpy as jnp
from jax import lax
from jax.experimental import pallas as pl
from jax.experimental.pallas import tpu as pltpu

N_META = 16
HEAD_DIM = 256
N_HEADS = 4
D_GROUP = N_HEADS * HEAD_DIM
CONV_WIDTH = 3
LN_EPS = 1e-5
RMS_EPS = 1e-6
DEEPNORM_ALPHA = 2.0 ** 0.25
GATE_COLS = 128
META_ROWS = 128
NEG_BIG = -1e30
EXIT_THRESHOLD = 110.0

F32 = jnp.float32
BF16 = jnp.bfloat16

VMEM_LIMIT = 56 * 1024 * 1024


def _layer_norm(x, g, b):
    mu = jnp.mean(x, axis=-1, keepdims=True)
    xc = x - mu
    var = jnp.mean(xc * xc, axis=-1, keepdims=True)
    return xc * lax.rsqrt(var + LN_EPS) * g + b


def _rms_norm(x, g):
    ms = jnp.mean(x * x, axis=-1, keepdims=True)
    return x * lax.rsqrt(ms + RMS_EPS) * g


def _log_sigmoid(z):
    return jnp.minimum(z, 0.0) - jnp.log1p(jnp.exp(-jnp.abs(z)))


def _split3(x):
    hi = x.astype(BF16)
    r1 = x - hi.astype(F32)
    mid = r1.astype(BF16)
    lo = (r1 - mid.astype(F32)).astype(BF16)
    return hi, mid, lo


def _resident(shape):
    nd = len(shape)
    return pl.BlockSpec(shape, lambda *_: (0,) * nd, pipeline_mode=pl.Buffered(1))


def _max_head_norms(yb):
    y = yb.astype(F32)
    sq = y * y
    rows = []
    for h in range(N_HEADS):
        n2 = jnp.sum(sq[:, h * HEAD_DIM:(h + 1) * HEAD_DIM], axis=-1, keepdims=True)
        rows.append(jnp.broadcast_to(jnp.max(n2, axis=0, keepdims=True), (1, GATE_COLS)))
    rows.append(jnp.zeros((8 - N_HEADS, GATE_COLS), F32))
    return jnp.concatenate(rows, axis=0)


def _in_proj_kernel(x_ref, g_ref, b_ref, w_ref, bf_ref, tril_ref,
                    qf_ref, kf_ref, vf_ref, qs_ref, ks_ref, vs_ref, nb_ref, kn_ref,
                    carry_ref, *, tiles_per_seq):
    i = pl.program_id(0)
    scale = 1.0 / math.sqrt(HEAD_DIM)
    hn = _layer_norm(x_ref[...], g_ref[...], b_ref[...])
    hb = hn.astype(BF16)
    outs = (qf_ref, kf_ref, vf_ref, qs_ref, ks_ref, vs_ref)
    for n, o_ref in enumerate(outs):
        y = jnp.dot(hb, w_ref[:, n * D_GROUP:(n + 1) * D_GROUP],
                    preferred_element_type=F32)
        if n in (0, 3):
            y = y * scale
        yb = y.astype(BF16)
        o_ref[...] = yb
        if n == 1:
            kn_ref[0] = _max_head_norms(yb)

    f_logit = jnp.dot(hb, w_ref[:, 6 * D_GROUP:], preferred_element_type=F32)
    log_f = _log_sigmoid(f_logit + bf_ref[...])

    @pl.when(i % tiles_per_seq == 0)
    def _():
        carry_ref[...] = jnp.zeros_like(carry_ref)

    tril = tril_ref[...]
    hi, mid, lo = _split3(log_f)
    c = (jnp.dot(tril, hi, preferred_element_type=F32)
         + jnp.dot(tril, mid, preferred_element_type=F32)
         + jnp.dot(tril, lo, preferred_element_type=F32))
    c = c + carry_ref[0:1, :]
    carry_ref[...] = jnp.broadcast_to(c[-1:, :], carry_ref.shape)
    nb_ref[...] = -jnp.transpose(c)[0:8, :]


def _in_proj(x2d, ln_g, ln_b, w_all, bf_row, *, rows_per_seq, tm):
    rows, d = x2d.shape
    n_tiles = rows // tm
    tril = jnp.tril(jnp.ones((tm, tm), BF16))
    qkv_shape = jax.ShapeDtypeStruct((rows, D_GROUP), BF16)
    qkv_spec = pl.BlockSpec((tm, D_GROUP), lambda i: (i, 0))
    return pl.pallas_call(
        functools.partial(_in_proj_kernel, tiles_per_seq=rows_per_seq // tm),
        grid=(n_tiles,),
        in_specs=[
            pl.BlockSpec((tm, d), lambda i: (i, 0)),
            _resident((1, d)), _resident((1, d)),
            _resident(w_all.shape),
            _resident((1, GATE_COLS)),
            _resident((tm, tm)),
        ],
        out_specs=[qkv_spec] * 6 + [pl.BlockSpec((8, tm), lambda i: (0, i)),
                                    pl.BlockSpec((1, 8, GATE_COLS), lambda i: (i, 0, 0))],
        out_shape=[qkv_shape] * 6 + [jax.ShapeDtypeStruct((8, rows), F32),
                                     jax.ShapeDtypeStruct((n_tiles, 8, GATE_COLS), F32)],
        scratch_shapes=[pltpu.VMEM((8, GATE_COLS), F32)],
        compiler_params=pltpu.CompilerParams(
            dimension_semantics=("arbitrary",), vmem_limit_bytes=VMEM_LIMIT),
        name="in_proj",
    )(x2d, ln_g, ln_b, w_all, bf_row, tril)


def _tile_ids(tq, tk):
    row = lax.broadcasted_iota(jnp.int32, (tq, tk), 0)
    col = lax.broadcasted_iota(jnp.int32, (tq, tk), 1)
    return row, col


def _qk(q, k):
    return lax.dot_general(q, k, (((1,), (1,)), ((), ())), preferred_element_type=F32)


def _fox_kernel(*refs, t, has_prefix):
    if has_prefix:
        (q_ref, k_ref, v_ref, nb_ref, kn_ref, nbe_ref, kp_ref, vp_ref, pb_ref, pkn_ref,
         o_ref) = refs
    else:
        q_ref, k_ref, v_ref, nb_ref, kn_ref, nbe_ref, o_ref = refs
    h = pl.program_id(1)
    i = pl.program_id(2)
    base = (pl.program_id(0) * N_HEADS + h) * pl.num_programs(2)
    q = q_ref[0]
    qf = q.astype(F32)
    q_norm = jnp.max(jnp.sqrt(jnp.sum(qf * qf, axis=-1, keepdims=True)))

    def reachable(key_norm, bias, m):
        return q_norm * key_norm + bias - jnp.min(m) >= -EXIT_THRESHOLD

    def block_reachable(j, m):
        jc = jnp.maximum(j, 0)
        return jnp.logical_and(j >= 0, reachable(kn_ref[base + jc], nbe_ref[base + jc], m))

    def update(carry, a, v):
        m, l, acc = carry
        m_new = jnp.maximum(m, jnp.max(a, axis=-1, keepdims=True))
        alpha = jnp.exp(m - m_new)
        p = jnp.exp(a - m_new)
        l = alpha * l + jnp.sum(p, axis=-1, keepdims=True)
        acc = alpha * acc + jnp.dot(p.astype(BF16), v, preferred_element_type=F32)
        return m_new, l, acc

    def block(j, carry, diagonal):
        start = pl.multiple_of(j * t, t)
        a = _qk(q, k_ref[0, pl.ds(start, t), :]) + nb_ref[0, 0, pl.ds(j, 1), :]
        if diagonal:
            row, col = _tile_ids(t, t)
            a = jnp.where(col <= row, a, NEG_BIG)
        return update(carry, a, v_ref[0, pl.ds(start, t), :])

    carry = (jnp.full((t, 1), NEG_BIG, F32), jnp.zeros((t, 1), F32),
             jnp.zeros((t, HEAD_DIM), F32))
    carry = block(i, carry, True)

    def body(state):
        j, _, carry = state
        carry = block(j, carry, False)
        return j - 1, block_reachable(j - 1, carry[0]).astype(jnp.int32), carry

    state = (i - 1, block_reachable(i - 1, carry[0]).astype(jnp.int32), carry)
    _, _, carry = lax.while_loop(lambda s: s[1] != 0, body, state)
    if has_prefix:
        def prefix(carry):
            a = _qk(q, kp_ref[...]) + pb_ref[pl.ds(h, 1), :]
            return update(carry, a, vp_ref[...])
        carry = lax.cond(reachable(pkn_ref[h], 0.0, carry[0]), prefix, lambda c: c, carry)
    _, l, acc = carry
    o_ref[0] = acc / l


def _sb_kernel(*refs, t, has_prefix):
    if has_prefix:
        q_ref, k_ref, v_ref, tri_ref, kp_ref, vp_ref, trip_ref, o_ref = refs
    else:
        q_ref, k_ref, v_ref, tri_ref, o_ref = refs
    i = pl.program_id(2)
    q = q_ref[0]

    def update(carry, z, v, tri, valid):
        run, acc = carry
        log_beta = _log_sigmoid(z)
        log_1m = log_beta - z
        if valid is not None:
            log_1m = jnp.where(valid, log_1m, 0.0)
        hi = log_1m.astype(BF16)
        lo = (log_1m - hi.astype(F32)).astype(BF16)
        rest = (jnp.dot(hi, tri, preferred_element_type=F32)
                + jnp.dot(lo, tri, preferred_element_type=F32))
        a = jnp.exp(log_beta + rest + run)
        if valid is not None:
            a = jnp.where(valid, a, 0.0)
        acc = acc + jnp.dot(a.astype(BF16), v, preferred_element_type=F32)
        run = run + jnp.sum(log_1m, axis=-1, keepdims=True)
        return run, acc

    def block(j, carry, diagonal):
        start = pl.multiple_of(j * t, t)
        z = _qk(q, k_ref[0, pl.ds(start, t), :])
        valid = None
        if diagonal:
            row, col = _tile_ids(t, t)
            valid = col < row
        return update(carry, z, v_ref[0, pl.ds(start, t), :], tri_ref[...], valid)

    def reachable(run):
        return jnp.max(run) >= -EXIT_THRESHOLD

    carry = (jnp.zeros((t, 1), F32), jnp.zeros((t, HEAD_DIM), F32))
    carry = block(i, carry, True)

    def body(state):
        j, _, carry = state
        carry = block(j, carry, False)
        go = jnp.logical_and(j >= 1, reachable(carry[0]))
        return j - 1, go.astype(jnp.int32), carry

    go = jnp.logical_and(i >= 1, reachable(carry[0]))
    _, _, carry = lax.while_loop(lambda s: s[1] != 0, body,
                                 (i - 1, go.astype(jnp.int32), carry))
    if has_prefix:
        def prefix(carry):
            _, col = _tile_ids(t, META_ROWS)
            return update(carry, _qk(q, kp_ref[...]), vp_ref[...], trip_ref[...],
                          col < N_META)
        carry = lax.cond(reachable(carry[0]), prefix, lambda c: c, carry)
    o_ref[0] = carry[1]


def _strict_lower(n):
    return jnp.tril(jnp.ones((n, n), BF16), k=-1)


def _attention_specs(batch, seq, t):
    q_spec = pl.BlockSpec((1, t, HEAD_DIM), lambda b, h, i: (b, i, h))
    kv_spec = pl.BlockSpec((1, seq, HEAD_DIM), lambda b, h, i: (b, 0, h))
    o_spec = pl.BlockSpec((1, t, HEAD_DIM), lambda b, h, i: (b, i, h))
    out_shape = jax.ShapeDtypeStruct((batch, seq, D_GROUP), F32)
    params = pltpu.CompilerParams(
        dimension_semantics=("arbitrary", "arbitrary", "arbitrary"),
        vmem_limit_bytes=VMEM_LIMIT)
    return q_spec, kv_spec, o_spec, out_shape, params


def _prefix_spec():
    return pl.BlockSpec((META_ROWS, HEAD_DIM), lambda b, h, i: (0, h))


def _fox_attention(q, k, v, nb, key_norm, prefix, *, t):
    batch, seq, _ = q.shape
    q_spec, kv_spec, o_spec, out_shape, params = _attention_specs(batch, seq, t)
    nb4 = nb.reshape(8, batch, seq // t, t)
    nb_end = jnp.transpose(nb4[:N_HEADS, :, :, t - 1], (1, 0, 2))
    smem = pl.BlockSpec(memory_space=pltpu.SMEM)
    in_specs = [q_spec, kv_spec, kv_spec,
                pl.BlockSpec((1, 1, seq // t, t), lambda b, h, i: (h, b, 0, 0)), smem, smem]
    args = [q, k, v, nb4, key_norm.reshape(-1), nb_end.reshape(-1)]
    if prefix is not None:
        in_specs += [_prefix_spec(), _prefix_spec(), _resident((8, META_ROWS)), smem]
        args += list(prefix)
    return pl.pallas_call(
        functools.partial(_fox_kernel, t=t, has_prefix=prefix is not None),
        grid=(batch, N_HEADS, seq // t),
        in_specs=in_specs, out_specs=o_spec, out_shape=out_shape,
        compiler_params=params, name="fox_attention",
    )(*args)


def _sb_attention(q, k, v, prefix, *, t):
    batch, seq, _ = q.shape
    q_spec, kv_spec, o_spec, out_shape, params = _attention_specs(batch, seq, t)
    in_specs = [q_spec, kv_spec, kv_spec, _resident((t, t))]
    args = [q, k, v, _strict_lower(t)]
    if prefix is not None:
        in_specs += [_prefix_spec(), _prefix_spec(), _resident((META_ROWS, META_ROWS))]
        args += list(prefix) + [_strict_lower(META_ROWS)]
    return pl.pallas_call(
        functools.partial(_sb_kernel, t=t, has_prefix=prefix is not None),
        grid=(batch, N_HEADS, seq // t),
        in_specs=in_specs, out_specs=o_spec, out_shape=out_shape,
        compiler_params=params, name="sb_attention",
    )(*args)


def _out_proj_kernel(of_ref, os_ref, x_ref, g0_ref, b0_ref, gf_ref, gs_ref, wo_ref,
                     g1_ref, b1_ref, h_ref):
    mf = _rms_norm(of_ref[...], gf_ref[...]).astype(BF16)
    ms = _rms_norm(os_ref[...], gs_ref[...]).astype(BF16)
    y = (jnp.dot(mf, wo_ref[0:D_GROUP, :], preferred_element_type=F32)
         + jnp.dot(ms, wo_ref[D_GROUP:, :], preferred_element_type=F32))
    h0 = _layer_norm(x_ref[...], g0_ref[...], b0_ref[...])
    h_ref[...] = _layer_norm(DEEPNORM_ALPHA * h0 + y, g1_ref[...], b1_ref[...])


def _out_proj(o_f, o_s, x2d, ln0_g, ln0_b, g_fox, g_sb, w_o, ln1_g, ln1_b, *, tm):
    rows, d = x2d.shape
    row_spec = lambda width: pl.BlockSpec((tm, width), lambda i: (i, 0))
    return pl.pallas_call(
        _out_proj_kernel,
        grid=(rows // tm,),
        in_specs=[row_spec(D_GROUP), row_spec(D_GROUP), row_spec(d),
                  _resident((1, d)), _resident((1, d)),
                  _resident((1, D_GROUP)), _resident((1, D_GROUP)),
                  _resident(w_o.shape), _resident((1, d)), _resident((1, d))],
        out_specs=row_spec(d),
        out_shape=jax.ShapeDtypeStruct((rows, d), F32),
        compiler_params=pltpu.CompilerParams(
            dimension_semantics=("arbitrary",), vmem_limit_bytes=VMEM_LIMIT),
        name="out_proj",
    )(o_f, o_s, x2d, ln0_g, ln0_b, g_fox, g_sb, w_o, ln1_g, ln1_b)


def _gate_kernel(h_ref, w_ref, g_ref):
    g_ref[...] = jnp.dot(h_ref[...].astype(BF16), w_ref[...], preferred_element_type=F32)


def _gate_rows(h_rows, w_gate, *, tf):
    rows, d = h_rows.shape
    d_ff = w_gate.shape[1]
    return pl.pallas_call(
        _gate_kernel,
        grid=(d_ff // tf,),
        in_specs=[_resident((rows, d)), pl.BlockSpec((d, tf), lambda f: (0, f))],
        out_specs=pl.BlockSpec((rows, tf), lambda f: (0, f)),
        out_shape=jax.ShapeDtypeStruct((rows, d_ff), F32),
        compiler_params=pltpu.CompilerParams(dimension_semantics=("arbitrary",)),
        name="gate_rows",
    )(h_rows, w_gate)


def _ffn_kernel(h_ref, wg_ref, wu_ref, wd_ref, cw_ref, cb_ref, gi_ref, g2_ref, b2_ref,
                o_ref, hb_ref, acc_ref, hist_ref, *, tm):
    i = pl.program_id(1)
    f = pl.program_id(2)

    @pl.when(f == 0)
    def _():
        hb_ref[...] = h_ref[0].astype(BF16)
        acc_ref[...] = jnp.zeros_like(acc_ref)

    @pl.when(i == 0)
    def _():
        hist_ref[f] = gi_ref[...]

    hb = hb_ref[...]
    g = jnp.dot(hb, wg_ref[...], preferred_element_type=F32)
    u = jnp.dot(hb, wu_ref[...], preferred_element_type=F32)
    hist = hist_ref[f]
    prev1 = hist[7:8, :]
    prev2 = hist[6:7, :]
    hist_ref[f] = g[tm - 8:, :]
    row = lax.broadcasted_iota(jnp.int32, g.shape, 0)
    g1 = jnp.where(row == 0, prev1, pltpu.roll(g, 1, 0))
    g2 = jnp.where(row == 0, prev2, jnp.where(row == 1, prev1, pltpu.roll(g, 2, 0)))
    cw = cw_ref[...]
    gc = cw[0:1, :] * g2 + cw[1:2, :] * g1 + cw[2:3, :] * g + cb_ref[...]
    act = gc * (1.0 / (1.0 + jnp.exp(-gc))) * u
    acc_ref[...] += jnp.dot(act.astype(BF16), wd_ref[...], preferred_element_type=F32)

    @pl.when(f == pl.num_programs(2) - 1)
    def _():
        o_ref[0] = _layer_norm(DEEPNORM_ALPHA * h_ref[0] + acc_ref[...],
                               g2_ref[...], b2_ref[...])


def _ffn(h1, w_gu, w_down, conv_w, conv_b, g_init, ln2_g, ln2_b, *, tm, tf):
    batch, seq, d = h1.shape
    d_ff = w_down.shape[0]
    n_f = d_ff // tf
    return pl.pallas_call(
        functools.partial(_ffn_kernel, tm=tm),
        grid=(batch, seq // tm, n_f),
        in_specs=[
            pl.BlockSpec((1, tm, d), lambda b, i, f: (b, i, 0)),
            pl.BlockSpec((d, tf), lambda b, i, f: (0, f)),
            pl.BlockSpec((d, tf), lambda b, i, f: (0, n_f + f)),
            pl.BlockSpec((tf, d), lambda b, i, f: (f, 0)),
            pl.BlockSpec((CONV_WIDTH, tf), lambda b, i, f: (0, f)),
            pl.BlockSpec((1, tf), lambda b, i, f: (0, f)),
            pl.BlockSpec((8, tf), lambda b, i, f: (0, f)),
            _resident((1, d)), _resident((1, d)),
        ],
        out_specs=pl.BlockSpec((1, tm, d), lambda b, i, f: (b, i, 0)),
        out_shape=jax.ShapeDtypeStruct((batch, seq, d), F32),
        scratch_shapes=[pltpu.VMEM((tm, d), BF16), pltpu.VMEM((tm, d), F32),
                        pltpu.VMEM((n_f, 8, tf), F32)],
        compiler_params=pltpu.CompilerParams(
            dimension_semantics=("arbitrary", "arbitrary", "arbitrary"),
            vmem_limit_bytes=VMEM_LIMIT),
        name="conv_ffn",
    )(h1, w_gu, w_gu, w_down, conv_w, conv_b, g_init, ln2_g, ln2_b)


def _pick(n, preferred):
    return preferred if n % preferred == 0 else n


def kernel(x, meta, ln0_g, ln0_b, w_in, b_f, g_fox, g_sb, w_o, ln1_g, ln1_b, w_gu, conv_w,
           conv_b, w_down, ln2_g, ln2_b):
    batch, seq, d = x.shape
    assert w_in.shape[0] == 1, "single layer"
    assert d == 2 * D_GROUP
    rows = batch * seq
    d_ff = w_down.shape[1]
    tm = _pick(seq, 512)
    t_att = _pick(seq, 256)
    tm_ffn = _pick(seq, 512)
    tf = _pick(d_ff, 512)

    row = lambda a: a.reshape(1, -1).astype(F32)
    w = w_in[0]
    gate_w = jnp.pad(w[:, 3 * D_GROUP:3 * D_GROUP + N_HEADS], ((0, 0), (0, GATE_COLS - N_HEADS)))
    w_all = jnp.concatenate([w[:, :3 * D_GROUP], w[:, 3 * D_GROUP + N_HEADS:], gate_w],
                            axis=1).astype(BF16)
    bf_row = jnp.pad(b_f[0].astype(F32), (0, GATE_COLS - N_HEADS)).reshape(1, GATE_COLS)
    w_o_b = w_o[0].astype(BF16)
    w_gu_b = w_gu[0].astype(BF16)
    w_down_b = w_down[0].astype(BF16)
    ln0 = (row(ln0_g), row(ln0_b))
    ln1 = (row(ln1_g[0]), row(ln1_b[0]))
    ln2 = (row(ln2_g[0]), row(ln2_b[0]))
    gf, gs = row(g_fox[0]), row(g_sb[0])

    def mixer(x2d, seqs, seq_len, tm_rows, t, meta_keys):
        qf, kf, vf, qs, ks, vs, nb, kn = _in_proj(x2d, *ln0, w_all, bf_row,
                                                  rows_per_seq=seq_len, tm=tm_rows)
        kn = jnp.sqrt(kn[:, :N_HEADS, 0]).reshape(seqs, seq_len // tm_rows, N_HEADS)
        kn = jnp.repeat(lax.cummax(jnp.transpose(kn, (0, 2, 1)), axis=2), tm_rows // t, axis=2)
        prefix_f = prefix_s = None
        if meta_keys is not None:
            kf_m, vf_m, nb_m, kn_m, ks_m, vs_m = meta_keys
            kn = jnp.maximum(kn, kn_m[0, :, :1])
            valid = jnp.arange(META_ROWS)[None, :] < N_META
            pbias = jnp.where(valid, nb_m - nb_m[:, N_META - 1:N_META], NEG_BIG)
            prefix_f = (kf_m, vf_m, pbias, kn_m.reshape(-1))
            prefix_s = (ks_m, vs_m)
        shape3 = (seqs, seq_len, D_GROUP)
        o_f = _fox_attention(qf.reshape(shape3), kf.reshape(shape3), vf.reshape(shape3),
                             nb, kn, prefix_f, t=t)
        o_s = _sb_attention(qs.reshape(shape3), ks.reshape(shape3), vs.reshape(shape3),
                            prefix_s, t=t)
        h1 = _out_proj(o_f.reshape(-1, D_GROUP), o_s.reshape(-1, D_GROUP), x2d, *ln0,
                       gf, gs, w_o_b, *ln1, tm=tm_rows)
        return h1, (kf, vf, nb, kn, ks, vs)

    meta_pad = jnp.zeros((META_ROWS, d), F32).at[:N_META].set(meta.astype(F32))
    h1_meta, meta_keys = mixer(meta_pad, 1, META_ROWS, META_ROWS, META_ROWS, None)
    g_meta = _gate_rows(h1_meta[:N_META], w_gu_b[:, :d_ff], tf=tf)
    g_init = g_meta[N_META - 8:N_META]

    h1, _ = mixer(x.reshape(rows, d).astype(F32), batch, seq, tm, t_att, meta_keys)
    out = _ffn(h1.reshape(batch, seq, d), w_gu_b, w_down_b, conv_w[0].astype(F32),
               row(conv_b[0]), g_init, *ln2, tm=tm_ffn, tf=tf)
    return out.astype(x.dtype)
```

```python
import functools
import math

import jax
import jax.numpy as jnp
from jax import lax
from jax.experimental import pallas as pl
from jax.experimental.pallas import tpu as pltpu

N_META = 16
HEAD_DIM = 256
N_HEADS = 4
D_GROUP = N_HEADS * HEAD_DIM
CONV_WIDTH = 3
LN_EPS = 1e-5
RMS_EPS = 1e-6
DEEPNORM_ALPHA = 2.0 ** 0.25
GATE_COLS = 128
META_ROWS = 128
NEG_BIG = -1e30
EXIT_THRESHOLD = 110.0

F32 = jnp.float32
BF16 = jnp.bfloat16

VMEM_LIMIT = 56 * 1024 * 1024


def _layer_norm(x, g, b):
    mu = jnp.mean(x, axis=-1, keepdims=True)
    xc = x - mu
    var = jnp.mean(xc * xc, axis=-1, keepdims=True)
    return xc * lax.rsqrt(var + LN_EPS) * g + b


def _rms_norm(x, g):
    ms = jnp.mean(x * x, axis=-1, keepdims=True)
    return x * lax.rsqrt(ms + RMS_EPS) * g


def _log_sigmoid(z):
    return jnp.minimum(z, 0.0) - jnp.log(1.0 + jnp.exp(-jnp.abs(z)))


def _split3(x):
    hi = x.astype(BF16)
    r1 = x - hi.astype(F32)
    mid = r1.astype(BF16)
    lo = (r1 - mid.astype(F32)).astype(BF16)
    return hi, mid, lo


def _resident(shape):
    nd = len(shape)
    return pl.BlockSpec(shape, lambda *_: (0,) * nd, pipeline_mode=pl.Buffered(1))


def _max_head_norms(yb):
    y = yb.astype(F32)
    sq = y * y
    rows = []
    for h in range(N_HEADS):
        n2 = jnp.sum(sq[:, h * HEAD_DIM:(h + 1) * HEAD_DIM], axis=-1, keepdims=True)
        rows.append(jnp.broadcast_to(jnp.max(n2, axis=0, keepdims=True), (1, GATE_COLS)))
    rows.append(jnp.zeros((8 - N_HEADS, GATE_COLS), F32))
    return jnp.concatenate(rows, axis=0)


def _in_proj_kernel(x_ref, g_ref, b_ref, wf_ref, ws_ref, wg_ref, bf_ref, tril_ref,
                    qf_ref, kf_ref, vf_ref, qs_ref, ks_ref, vs_ref, nb_ref, kn_ref,
                    carry_ref, *, tiles_per_seq):
    i = pl.program_id(0)
    scale = 1.0 / math.sqrt(HEAD_DIM)
    hn = _layer_norm(x_ref[...], g_ref[...], b_ref[...])
    hb = hn.astype(BF16)
    outs = ((qf_ref, kf_ref, vf_ref), (qs_ref, ks_ref, vs_ref))
    for w_ref, group in zip((wf_ref, ws_ref), outs):
        for n, o_ref in enumerate(group):
            y = jnp.dot(hb, w_ref[:, n * D_GROUP:(n + 1) * D_GROUP],
                        preferred_element_type=F32)
            if n == 0:
                y = y * scale
            yb = y.astype(BF16)
            o_ref[...] = yb
            if o_ref is kf_ref:
                kn_ref[0] = _max_head_norms(yb)

    f_logit = jnp.dot(hb, wg_ref[...], preferred_element_type=F32)
    log_f = _log_sigmoid(f_logit + bf_ref[...])

    @pl.when(i % tiles_per_seq == 0)
    def _():
        carry_ref[...] = jnp.zeros_like(carry_ref)

    tril = tril_ref[...]
    hi, mid, lo = _split3(log_f)
    c = (jnp.dot(tril, hi, preferred_element_type=F32)
         + jnp.dot(tril, mid, preferred_element_type=F32)
         + jnp.dot(tril, lo, preferred_element_type=F32))
    c = c + carry_ref[0:1, :]
    carry_ref[...] = jnp.broadcast_to(c[-1:, :], carry_ref.shape)
    nb_ref[...] = -jnp.transpose(c)[0:8, :]


def _in_proj(x2d, ln_g, ln_b, w_fox, w_sb, w_gate, bf_row, *, rows_per_seq, tm):
    rows, d = x2d.shape
    n_tiles = rows // tm
    tril = jnp.tril(jnp.ones((tm, tm), BF16))
    qkv_shape = jax.ShapeDtypeStruct((rows, D_GROUP), BF16)
    qkv_spec = pl.BlockSpec((tm, D_GROUP), lambda i: (i, 0))
    return pl.pallas_call(
        functools.partial(_in_proj_kernel, tiles_per_seq=rows_per_seq // tm),
        grid=(n_tiles,),
        in_specs=[
            pl.BlockSpec((tm, d), lambda i: (i, 0)),
            _resident((1, d)), _resident((1, d)),
            _resident(w_fox.shape), _resident(w_sb.shape), _resident(w_gate.shape),
            _resident((1, GATE_COLS)),
            _resident((tm, tm)),
        ],
        out_specs=[qkv_spec] * 6 + [pl.BlockSpec((8, tm), lambda i: (0, i)),
                                    pl.BlockSpec((1, 8, GATE_COLS), lambda i: (i, 0, 0))],
        out_shape=[qkv_shape] * 6 + [jax.ShapeDtypeStruct((8, rows), F32),
                                     jax.ShapeDtypeStruct((n_tiles, 8, GATE_COLS), F32)],
        scratch_shapes=[pltpu.VMEM((8, GATE_COLS), F32)],
        compiler_params=pltpu.CompilerParams(
            dimension_semantics=("arbitrary",), vmem_limit_bytes=VMEM_LIMIT),
        name="in_proj",
    )(x2d, ln_g, ln_b, w_fox, w_sb, w_gate, bf_row, tril)


def _tile_ids(tq, tk):
    row = lax.broadcasted_iota(jnp.int32, (tq, tk), 0)
    col = lax.broadcasted_iota(jnp.int32, (tq, tk), 1)
    return row, col


def _row_groups(t):
    return 2 if t % 512 == 0 else 1


def _qk(q, k):
    return lax.dot_general(q, k, (((1,), (1,)), ((), ())), preferred_element_type=F32)


def _fox_kernel(*refs, t, has_prefix):
    if has_prefix:
        (q_ref, k_ref, v_ref, nb_ref, kn_ref, nbe_ref, kp_ref, vp_ref, pb_ref, pkn_ref,
         o_ref) = refs
    else:
        q_ref, k_ref, v_ref, nb_ref, kn_ref, nbe_ref, o_ref = refs
    h = pl.program_id(1)
    i = pl.program_id(2)
    base = (pl.program_id(0) * N_HEADS + h) * pl.num_programs(2)
    n_split = _row_groups(t)
    hs = t // n_split
    q_parts = [q_ref[0, r * hs:(r + 1) * hs, :] for r in range(n_split)]

    def max_norm(qp):
        qf = qp.astype(F32)
        return jnp.max(jnp.sqrt(jnp.sum(qf * qf, axis=-1, keepdims=True)))

    q_norm = functools.reduce(jnp.maximum, [max_norm(qp) for qp in q_parts])

    def reachable(key_norm, bias, carry):
        min_m = functools.reduce(jnp.minimum, [jnp.min(c[0]) for c in carry])
        return q_norm * key_norm + bias - min_m >= -EXIT_THRESHOLD

    def block_reachable(j, carry):
        jc = jnp.maximum(j, 0)
        return jnp.logical_and(j >= 0,
                               reachable(kn_ref[base + jc], nbe_ref[base + jc], carry))

    def update(c, qp, k, v, bias, mask):
        m, l, acc = c
        a = _qk(qp, k) + bias
        if mask is not None:
            a = jnp.where(mask, a, NEG_BIG)
        m_new = jnp.maximum(m, jnp.max(a, axis=-1, keepdims=True))
        alpha = jnp.exp(m - m_new)
        p = jnp.exp(a - m_new)
        l = alpha * l + jnp.sum(p, axis=-1, keepdims=True)
        acc = alpha * acc + jnp.dot(p.astype(BF16), v, preferred_element_type=F32)
        return m_new, l, acc

    def block(j, carry):
        start = pl.multiple_of(j * t, t)
        k = k_ref[0, pl.ds(start, t), :]
        v = v_ref[0, pl.ds(start, t), :]
        bias = nb_ref[0, 0, pl.ds(j, 1), :]
        return tuple(update(c, qp, k, v, bias, None) for c, qp in zip(carry, q_parts))

    def diagonal(carry):
        start = pl.multiple_of(i * t, t)
        out = []
        for r, (c, qp) in enumerate(zip(carry, q_parts)):
            width = (r + 1) * hs
            row, col = _tile_ids(hs, width)
            out.append(update(c, qp, k_ref[0, pl.ds(start, width), :],
                              v_ref[0, pl.ds(start, width), :],
                              nb_ref[0, 0, pl.ds(i, 1), 0:width], col <= row + r * hs))
        return tuple(out)

    carry = tuple((jnp.full((hs, 1), NEG_BIG, F32), jnp.zeros((hs, 1), F32),
                   jnp.zeros((hs, HEAD_DIM), F32)) for _ in range(n_split))
    carry = diagonal(carry)

    def body(state):
        j, _, carry = state
        carry = block(j, carry)
        return j - 1, block_reachable(j - 1, carry).astype(jnp.int32), carry

    state = (i - 1, block_reachable(i - 1, carry).astype(jnp.int32), carry)
    _, _, carry = lax.while_loop(lambda s: s[1] != 0, body, state)
    if has_prefix:
        def prefix(carry):
            k, v, bias = kp_ref[...], vp_ref[...], pb_ref[pl.ds(h, 1), :]
            return tuple(update(c, qp, k, v, bias, None) for c, qp in zip(carry, q_parts))
        carry = lax.cond(reachable(pkn_ref[h], 0.0, carry), prefix, lambda c: c, carry)
    for r, (_, l, acc) in enumerate(carry):
        o_ref[0, r * hs:(r + 1) * hs, :] = acc / l


def _sb_kernel(*refs, tq, tk, single_tile, has_prefix):
    if has_prefix:
        q_ref, k_ref, v_ref, tri_ref, kp_ref, vp_ref, trip_ref, o_ref = refs
    else:
        q_ref, k_ref, v_ref, tri_ref, o_ref = refs
    i = pl.program_id(2)
    n_split = tq // tk
    first = i * n_split
    q_parts = [q_ref[0, r * tk:(r + 1) * tk, :] for r in range(n_split)]
    tri = tri_ref[...]

    def update(carry, z, v, tri, valid):
        run, acc = carry
        chunk = tri.shape[0]
        log_beta = _log_sigmoid(z)
        log_1m = log_beta - z
        if valid is not None:
            log_1m = jnp.where(valid, log_1m, 0.0)
        rests = []
        for c in reversed(range(z.shape[1] // chunk)):
            x = log_1m[:, c * chunk:(c + 1) * chunk]
            hi = x.astype(BF16)
            lo = (x - hi.astype(F32)).astype(BF16)
            rests.append(jnp.dot(hi, tri, preferred_element_type=F32)
                         + jnp.dot(lo, tri, preferred_element_type=F32) + run)
            run = run + jnp.sum(x, axis=-1, keepdims=True)
        rest = rests[0] if len(rests) == 1 else jnp.concatenate(rests[::-1], axis=1)
        a = jnp.exp(log_beta + rest)
        if valid is not None:
            a = jnp.where(valid, a, 0.0)
        acc = acc + jnp.dot(a.astype(BF16), v, preferred_element_type=F32)
        return run, acc

    def reachable(c):
        return jnp.max(c[0]) >= -EXIT_THRESHOLD

    def own_block(c, qp, j):
        start = pl.multiple_of(j * tk, tk)
        row, col = _tile_ids(tk, tk)
        return update(c, _qk(qp, k_ref[0, pl.ds(start, tk), :]),
                      v_ref[0, pl.ds(start, tk), :], tri, col < row)

    def own_and_previous_block(c, qp, j):
        start = pl.multiple_of((j - 1) * tk, tk)
        row, col = _tile_ids(tk, 2 * tk)
        return update(c, _qk(qp, k_ref[0, pl.ds(start, 2 * tk), :]),
                      v_ref[0, pl.ds(start, 2 * tk), :], tri, col < row + tk)

    def spans(carry, with_previous):
        out = []
        for r, (c, qp) in enumerate(zip(carry, q_parts)):
            if r > 0 or with_previous:
                out.append(own_and_previous_block(c, qp, first + r))
            else:
                out.append(own_block(c, qp, first))
        return tuple(out)

    carry = tuple((jnp.zeros((tk, 1), F32), jnp.zeros((tk, HEAD_DIM), F32))
                  for _ in range(n_split))
    if single_tile:
        carry = spans(carry, False)
    else:
        carry = lax.cond(i > 0, lambda c: spans(c, True), lambda c: spans(c, False), carry)

    def walk_flags(n, carry):
        return [jnp.logical_and(first + r - 2 - n >= 0, reachable(c))
                for r, c in enumerate(carry)]

    def body(state):
        n, _, carry = state
        out = []
        for r, (c, qp, go) in enumerate(zip(carry, q_parts, walk_flags(n, carry))):
            def step(c, qp=qp, j=first + r - 2 - n):
                start = pl.multiple_of(jnp.maximum(j, 0) * tk, tk)
                return update(c, _qk(qp, k_ref[0, pl.ds(start, tk), :]),
                              v_ref[0, pl.ds(start, tk), :], tri, None)
            out.append(lax.cond(go, step, lambda c: c, c))
        carry = tuple(out)
        go = functools.reduce(jnp.logical_or, walk_flags(n + 1, carry))
        return n + 1, go.astype(jnp.int32), carry

    go = functools.reduce(jnp.logical_or, walk_flags(0, carry))
    _, _, carry = lax.while_loop(lambda s: s[1] != 0, body,
                                 (jnp.int32(0), go.astype(jnp.int32), carry))
    if has_prefix:
        def prefix(carry):
            _, col = _tile_ids(tk, META_ROWS)
            k, v, trip = kp_ref[...], vp_ref[...], trip_ref[...]
            return tuple(update(c, _qk(qp, k), v, trip, col < N_META)
                         for c, qp in zip(carry, q_parts))
        go = functools.reduce(jnp.logical_or, [reachable(c) for c in carry])
        carry = lax.cond(go, prefix, lambda c: c, carry)
    for r, (_, acc) in enumerate(carry):
        o_ref[0, r * tk:(r + 1) * tk, :] = acc


def _strict_lower(n):
    return jnp.tril(jnp.ones((n, n), BF16), k=-1)


def _attention_specs(batch, seq, t):
    q_spec = pl.BlockSpec((1, t, HEAD_DIM), lambda b, h, i: (b, i, h))
    kv_spec = pl.BlockSpec((1, seq, HEAD_DIM), lambda b, h, i: (b, 0, h))
    o_spec = pl.BlockSpec((1, t, HEAD_DIM), lambda b, h, i: (b, i, h))
    out_shape = jax.ShapeDtypeStruct((batch, seq, D_GROUP), F32)
    params = pltpu.CompilerParams(
        dimension_semantics=("arbitrary", "arbitrary", "arbitrary"),
        vmem_limit_bytes=VMEM_LIMIT)
    return q_spec, kv_spec, o_spec, out_shape, params


def _prefix_spec():
    return pl.BlockSpec((META_ROWS, HEAD_DIM), lambda b, h, i: (0, h))


def _fox_attention(q, k, v, nb, key_norm, prefix, *, t):
    batch, seq, _ = q.shape
    q_spec, kv_spec, o_spec, out_shape, params = _attention_specs(batch, seq, t)
    nb4 = nb.reshape(8, batch, seq // t, t)
    nb_end = jnp.transpose(nb4[:N_HEADS, :, :, t - 1], (1, 0, 2))
    smem = pl.BlockSpec(memory_space=pltpu.SMEM)
    in_specs = [q_spec, kv_spec, kv_spec,
                pl.BlockSpec((1, 1, seq // t, t), lambda b, h, i: (h, b, 0, 0)), smem, smem]
    args = [q, k, v, nb4, key_norm.reshape(-1), nb_end.reshape(-1)]
    if prefix is not None:
        in_specs += [_prefix_spec(), _prefix_spec(), _resident((8, META_ROWS)), smem]
        args += list(prefix)
    return pl.pallas_call(
        functools.partial(_fox_kernel, t=t, has_prefix=prefix is not None),
        grid=(batch, N_HEADS, seq // t),
        in_specs=in_specs, out_specs=o_spec, out_shape=out_shape,
        compiler_params=params, name="fox_attention",
    )(*args)


def _sb_attention(q, k, v, prefix, *, tq, tk):
    batch, seq, _ = q.shape
    q_spec, kv_spec, o_spec, out_shape, params = _attention_specs(batch, seq, tq)
    in_specs = [q_spec, kv_spec, kv_spec, _resident((tk, tk))]
    args = [q, k, v, _strict_lower(tk)]
    if prefix is not None:
        in_specs += [_prefix_spec(), _prefix_spec(), _resident((META_ROWS, META_ROWS))]
        args += list(prefix) + [_strict_lower(META_ROWS)]
    return pl.pallas_call(
        functools.partial(_sb_kernel, tq=tq, tk=tk, single_tile=seq == tq,
                          has_prefix=prefix is not None),
        grid=(batch, N_HEADS, seq // tq),
        in_specs=in_specs, out_specs=o_spec, out_shape=out_shape,
        compiler_params=params, name="sb_attention",
    )(*args)


def _out_proj_kernel(of_ref, os_ref, x_ref, g0_ref, b0_ref, gf_ref, gs_ref, wo_ref,
                     g1_ref, b1_ref, h_ref):
    mf = _rms_norm(of_ref[...], gf_ref[...]).astype(BF16)
    ms = _rms_norm(os_ref[...], gs_ref[...]).astype(BF16)
    y = (jnp.dot(mf, wo_ref[0:D_GROUP, :], preferred_element_type=F32)
         + jnp.dot(ms, wo_ref[D_GROUP:, :], preferred_element_type=F32))
    h0 = _layer_norm(x_ref[...], g0_ref[...], b0_ref[...])
    h_ref[...] = _layer_norm(DEEPNORM_ALPHA * h0 + y, g1_ref[...], b1_ref[...])


def _out_proj(o_f, o_s, x2d, ln0_g, ln0_b, g_fox, g_sb, w_o, ln1_g, ln1_b, *, tm):
    rows, d = x2d.shape
    row_spec = lambda width: pl.BlockSpec((tm, width), lambda i: (i, 0))
    return pl.pallas_call(
        _out_proj_kernel,
        grid=(rows // tm,),
        in_specs=[row_spec(D_GROUP), row_spec(D_GROUP), row_spec(d),
                  _resident((1, d)), _resident((1, d)),
                  _resident((1, D_GROUP)), _resident((1, D_GROUP)),
                  _resident(w_o.shape), _resident((1, d)), _resident((1, d))],
        out_specs=row_spec(d),
        out_shape=jax.ShapeDtypeStruct((rows, d), F32),
        compiler_params=pltpu.CompilerParams(
            dimension_semantics=("arbitrary",), vmem_limit_bytes=VMEM_LIMIT),
        name="out_proj",
    )(o_f, o_s, x2d, ln0_g, ln0_b, g_fox, g_sb, w_o, ln1_g, ln1_b)


def _gate_kernel(h_ref, w_ref, g_ref):
    g_ref[...] = jnp.dot(h_ref[...].astype(BF16), w_ref[...], preferred_element_type=F32)


def _gate_rows(h_rows, w_gu, d_ff, *, tf):
    rows, d = h_rows.shape
    return pl.pallas_call(
        _gate_kernel,
        grid=(d_ff // tf,),
        in_specs=[_resident((rows, d)), pl.BlockSpec((d, tf), lambda f: (0, f))],
        out_specs=pl.BlockSpec((rows, tf), lambda f: (0, f)),
        out_shape=jax.ShapeDtypeStruct((rows, d_ff), F32),
        compiler_params=pltpu.CompilerParams(dimension_semantics=("arbitrary",)),
        name="gate_rows",
    )(h_rows, w_gu)


def _ffn_kernel(h_ref, wg_ref, wu_ref, wd_ref, cw_ref, cb_ref, gi_ref, g2_ref, b2_ref,
                o_ref, hb_ref, acc_ref, hist_ref, *, tm):
    i = pl.program_id(1)
    f = pl.program_id(2)

    @pl.when(f == 0)
    def _():
        hb_ref[...] = h_ref[0].astype(BF16)
        acc_ref[...] = jnp.zeros_like(acc_ref)

    @pl.when(i == 0)
    def _():
        hist_ref[f] = gi_ref[...]

    hb = hb_ref[...]
    g = jnp.dot(hb, wg_ref[...], preferred_element_type=F32)
    u = jnp.dot(hb, wu_ref[...], preferred_element_type=F32)
    hist = hist_ref[f]
    prev1 = hist[7:8, :]
    prev2 = hist[6:7, :]
    hist_ref[f] = g[tm - 8:, :]
    row = lax.broadcasted_iota(jnp.int32, g.shape, 0)
    g1 = jnp.where(row == 0, prev1, pltpu.roll(g, 1, 0))
    g2 = jnp.where(row == 0, prev2, jnp.where(row == 1, prev1, pltpu.roll(g, 2, 0)))
    cw = cw_ref[...]
    gc = cw[0:1, :] * g2 + cw[1:2, :] * g1 + cw[2:3, :] * g + cb_ref[...]
    act = gc * (1.0 / (1.0 + jnp.exp(-gc))) * u
    acc_ref[...] += jnp.dot(act.astype(BF16), wd_ref[...], preferred_element_type=F32)

    @pl.when(f == pl.num_programs(2) - 1)
    def _():
        o_ref[0] = _layer_norm(DEEPNORM_ALPHA * h_ref[0] + acc_ref[...],
                               g2_ref[...], b2_ref[...])


def _ffn(h1, w_gu, w_down, conv_w, conv_b, g_init, ln2_g, ln2_b, *, tm, tf):
    batch, seq, d = h1.shape
    d_ff = w_down.shape[0]
    n_f = d_ff // tf
    return pl.pallas_call(
        functools.partial(_ffn_kernel, tm=tm),
        grid=(batch, seq // tm, n_f),
        in_specs=[
            pl.BlockSpec((1, tm, d), lambda b, i, f: (b, i, 0)),
            pl.BlockSpec((d, tf), lambda b, i, f: (0, f)),
            pl.BlockSpec((d, tf), lambda b, i, f: (0, n_f + f)),
            pl.BlockSpec((tf, d), lambda b, i, f: (f, 0)),
            pl.BlockSpec((CONV_WIDTH, tf), lambda b, i, f: (0, f)),
            pl.BlockSpec((1, tf), lambda b, i, f: (0, f)),
            pl.BlockSpec((8, tf), lambda b, i, f: (0, f)),
            _resident((1, d)), _resident((1, d)),
        ],
        out_specs=pl.BlockSpec((1, tm, d), lambda b, i, f: (b, i, 0)),
        out_shape=jax.ShapeDtypeStruct((batch, seq, d), F32),
        scratch_shapes=[pltpu.VMEM((tm, d), BF16), pltpu.VMEM((tm, d), F32),
                        pltpu.VMEM((n_f, 8, tf), F32)],
        compiler_params=pltpu.CompilerParams(
            dimension_semantics=("arbitrary", "arbitrary", "arbitrary"),
            vmem_limit_bytes=VMEM_LIMIT),
        name="conv_ffn",
    )(h1, w_gu, w_gu, w_down, conv_w, conv_b, g_init, ln2_g, ln2_b)


def _pick(n, preferred):
    return preferred if n % preferred == 0 else n


def kernel(x, meta, ln0_g, ln0_b, w_in, b_f, g_fox, g_sb, w_o, ln1_g, ln1_b, w_gu, conv_w,
           conv_b, w_down, ln2_g, ln2_b):
    batch, seq, d = x.shape
    assert w_in.shape[0] == 1, "single layer"
    assert d == 2 * D_GROUP
    rows = batch * seq
    d_ff = w_down.shape[1]
    tm = _pick(seq, 512)
    tk_sb = _pick(seq, 256)
    tm_ffn = _pick(seq, 512)
    tf = _pick(d_ff, 512)

    row = lambda a: a.reshape(1, -1).astype(F32)
    w = w_in[0]
    w_fox = w[:, :3 * D_GROUP].astype(BF16)
    w_sb = w[:, 3 * D_GROUP + N_HEADS:].astype(BF16)
    w_gate = jnp.pad(w[:, 3 * D_GROUP:3 * D_GROUP + N_HEADS],
                     ((0, 0), (0, GATE_COLS - N_HEADS))).astype(BF16)
    bf_row = jnp.pad(b_f[0].astype(F32), (0, GATE_COLS - N_HEADS)).reshape(1, GATE_COLS)
    w_o_b = w_o[0].astype(BF16)
    w_gu_b = w_gu[0].astype(BF16)
    w_down_b = w_down[0].astype(BF16)
    ln0 = (row(ln0_g), row(ln0_b))
    ln1 = (row(ln1_g[0]), row(ln1_b[0]))
    ln2 = (row(ln2_g[0]), row(ln2_b[0]))
    gf, gs = row(g_fox[0]), row(g_sb[0])

    def mixer(x2d, seqs, seq_len, t, tk, meta_keys):
        qf, kf, vf, qs, ks, vs, nb, kn = _in_proj(x2d, *ln0, w_fox, w_sb, w_gate, bf_row,
                                                  rows_per_seq=seq_len, tm=t)
        kn = jnp.sqrt(kn[:, :N_HEADS, 0]).reshape(seqs, seq_len // t, N_HEADS)
        kn = lax.cummax(jnp.transpose(kn, (0, 2, 1)), axis=2)
        prefix_f = prefix_s = None
        if meta_keys is not None:
            kf_m, vf_m, nb_m, kn_m, ks_m, vs_m = meta_keys
            kn = jnp.maximum(kn, kn_m[0, :, :1])
            valid = jnp.arange(META_ROWS)[None, :] < N_META
            pbias = jnp.where(valid, nb_m - nb_m[:, N_META - 1:N_META], NEG_BIG)
            prefix_f = (kf_m, vf_m, pbias, kn_m.reshape(-1))
            prefix_s = (ks_m, vs_m)
        shape3 = (seqs, seq_len, D_GROUP)
        o_f = _fox_attention(qf.reshape(shape3), kf.reshape(shape3), vf.reshape(shape3),
                             nb, kn, prefix_f, t=t)
        o_s = _sb_attention(qs.reshape(shape3), ks.reshape(shape3), vs.reshape(shape3),
                            prefix_s, tq=t, tk=tk)
        h1 = _out_proj(o_f.reshape(-1, D_GROUP), o_s.reshape(-1, D_GROUP), x2d, *ln0,
                       gf, gs, w_o_b, *ln1, tm=t)
        return h1, (kf, vf, nb, kn, ks, vs)

    meta_pad = jnp.zeros((META_ROWS, d), F32).at[:N_META].set(meta.astype(F32))
    h1_meta, meta_keys = mixer(meta_pad, 1, META_ROWS, META_ROWS, META_ROWS, None)
    g_meta = _gate_rows(h1_meta[:N_META], w_gu_b, d_ff, tf=tf)
    g_init = g_meta[N_META - 8:N_META]

    h1, _ = mixer(x.reshape(rows, d).astype(F32), batch, seq, tm, tk_sb, meta_keys)
    out = _ffn(h1.reshape(batch, seq, d), w_gu_b, w_down_b, conv_w[0].astype(F32),
               row(conv_b[0]), g_init, *ln2, tm=tm_ffn, tf=tf)
    return out.astype(x.dtype)
```

```python
import functools
import math

import jax
import jax.numpy as jnp
from jax import lax
from jax.experimental import pallas as pl
from jax.experimental.pallas import tpu as pltpu

N_META = 16
HEAD_DIM = 256
N_HEADS = 4
D_GROUP = N_HEADS * HEAD_DIM
CONV_WIDTH = 3
LN_EPS = 1e-5
RMS_EPS = 1e-6
DEEPNORM_ALPHA = 2.0 ** 0.25
GATE_COLS = 128
META_ROWS = 128
NEG_BIG = -1e30
EXIT_THRESHOLD = 110.0
FIXED_MAX_GAP = 40.0

F32 = jnp.float32
BF16 = jnp.bfloat16

VMEM_LIMIT = 56 * 1024 * 1024


def _layer_norm(x, g, b):
    mu = jnp.mean(x, axis=-1, keepdims=True)
    xc = x - mu
    var = jnp.mean(xc * xc, axis=-1, keepdims=True)
    return xc * lax.rsqrt(var + LN_EPS) * g + b


def _rms_norm(x, g):
    ms = jnp.mean(x * x, axis=-1, keepdims=True)
    return x * lax.rsqrt(ms + RMS_EPS) * g


def _log_sigmoid(z):
    return jnp.minimum(z, 0.0) - jnp.log(1.0 + jnp.exp(-jnp.abs(z)))


def _split3(x):
    hi = x.astype(BF16)
    r1 = x - hi.astype(F32)
    mid = r1.astype(BF16)
    lo = (r1 - mid.astype(F32)).astype(BF16)
    return hi, mid, lo


def _resident(shape):
    nd = len(shape)
    return pl.BlockSpec(shape, lambda *_: (0,) * nd, pipeline_mode=pl.Buffered(1))


def _max_head_norms(yb):
    y = yb.astype(F32)
    sq = y * y
    rows = []
    for h in range(N_HEADS):
        n2 = jnp.sum(sq[:, h * HEAD_DIM:(h + 1) * HEAD_DIM], axis=-1, keepdims=True)
        rows.append(jnp.broadcast_to(jnp.max(n2, axis=0, keepdims=True), (1, GATE_COLS)))
    rows.append(jnp.zeros((8 - N_HEADS, GATE_COLS), F32))
    return jnp.concatenate(rows, axis=0)


def _in_proj_kernel(x_ref, g_ref, b_ref, wf_ref, ws_ref, wg_ref, bf_ref, tril_ref,
                    qf_ref, kf_ref, vf_ref, qs_ref, ks_ref, vs_ref, nb_ref, kn_ref,
                    carry_ref, *, tiles_per_seq):
    i = pl.program_id(0)
    scale = 1.0 / math.sqrt(HEAD_DIM)
    hn = _layer_norm(x_ref[...], g_ref[...], b_ref[...])
    hb = hn.astype(BF16)
    outs = ((qf_ref, kf_ref, vf_ref), (qs_ref, ks_ref, vs_ref))
    for w_ref, group in zip((wf_ref, ws_ref), outs):
        for n, o_ref in enumerate(group):
            y = jnp.dot(hb, w_ref[:, n * D_GROUP:(n + 1) * D_GROUP],
                        preferred_element_type=F32)
            if n == 0:
                y = y * scale
            yb = y.astype(BF16)
            o_ref[...] = yb
            if o_ref is kf_ref:
                kn_ref[0] = _max_head_norms(yb)

    f_logit = jnp.dot(hb, wg_ref[...], preferred_element_type=F32)
    log_f = _log_sigmoid(f_logit + bf_ref[...])

    @pl.when(i % tiles_per_seq == 0)
    def _():
        carry_ref[...] = jnp.zeros_like(carry_ref)

    tril = tril_ref[...]
    hi, mid, lo = _split3(log_f)
    c = (jnp.dot(tril, hi, preferred_element_type=F32)
         + jnp.dot(tril, mid, preferred_element_type=F32)
         + jnp.dot(tril, lo, preferred_element_type=F32))
    c = c + carry_ref[0:1, :]
    carry_ref[...] = jnp.broadcast_to(c[-1:, :], carry_ref.shape)
    nb_ref[...] = -jnp.transpose(c)[0:8, :]


def _in_proj(x2d, ln_g, ln_b, w_fox, w_sb, w_gate, bf_row, *, rows_per_seq, tm):
    rows, d = x2d.shape
    n_tiles = rows // tm
    tril = jnp.tril(jnp.ones((tm, tm), BF16))
    qkv_shape = jax.ShapeDtypeStruct((rows, D_GROUP), BF16)
    qkv_spec = pl.BlockSpec((tm, D_GROUP), lambda i: (i, 0))
    return pl.pallas_call(
        functools.partial(_in_proj_kernel, tiles_per_seq=rows_per_seq // tm),
        grid=(n_tiles,),
        in_specs=[
            pl.BlockSpec((tm, d), lambda i: (i, 0)),
            _resident((1, d)), _resident((1, d)),
            _resident(w_fox.shape), _resident(w_sb.shape), _resident(w_gate.shape),
            _resident((1, GATE_COLS)),
            _resident((tm, tm)),
        ],
        out_specs=[qkv_spec] * 6 + [pl.BlockSpec((8, tm), lambda i: (0, i)),
                                    pl.BlockSpec((1, 8, GATE_COLS), lambda i: (i, 0, 0))],
        out_shape=[qkv_shape] * 6 + [jax.ShapeDtypeStruct((8, rows), F32),
                                     jax.ShapeDtypeStruct((n_tiles, 8, GATE_COLS), F32)],
        scratch_shapes=[pltpu.VMEM((8, GATE_COLS), F32)],
        compiler_params=pltpu.CompilerParams(
            dimension_semantics=("arbitrary",), vmem_limit_bytes=VMEM_LIMIT),
        name="in_proj",
    )(x2d, ln_g, ln_b, w_fox, w_sb, w_gate, bf_row, tril)


def _tile_ids(tq, tk):
    row = lax.broadcasted_iota(jnp.int32, (tq, tk), 0)
    col = lax.broadcasted_iota(jnp.int32, (tq, tk), 1)
    return row, col


def _row_groups(t):
    return 2 if t % 512 == 0 else 1


def _qk(q, k):
    return lax.dot_general(q, k, (((1,), (1,)), ((), ())), preferred_element_type=F32)


def _fox_kernel(*refs, t, has_prefix):
    if has_prefix:
        (q_ref, k_ref, v_ref, nb_ref, kn_ref, nbe_ref, kp_ref, vp_ref, pb_ref, pkn_ref,
         o_ref, m_ref, l_ref, acc_ref) = refs
    else:
        q_ref, k_ref, v_ref, nb_ref, kn_ref, nbe_ref, o_ref, m_ref, l_ref, acc_ref = refs
    h = pl.program_id(1)
    i = pl.program_id(2)
    base = (pl.program_id(0) * N_HEADS + h) * pl.num_programs(2)
    n_split = _row_groups(t)
    hs = t // n_split
    groups = [slice(r * hs, (r + 1) * hs) for r in range(n_split)]
    q_parts = [q_ref[0, g, :] for g in groups]

    def max_norm(qp):
        qf = qp.astype(F32)
        return jnp.max(jnp.sqrt(jnp.sum(qf * qf, axis=-1, keepdims=True)))

    q_norm = functools.reduce(jnp.maximum, [max_norm(qp) for qp in q_parts])

    def logit_gap(key_norm, bias):
        return q_norm * key_norm + bias - jnp.min(m_ref[...])

    def block_gap(j):
        jc = jnp.maximum(j, 0)
        return logit_gap(kn_ref[base + jc], nbe_ref[base + jc])

    def reachable(j, gap):
        return jnp.logical_and(j >= 0, gap >= -EXIT_THRESHOLD).astype(jnp.int32)

    def first_update(g, a, v):
        m = jnp.max(a, axis=-1, keepdims=True)
        p = jnp.exp(a - m)
        m_ref[g, :] = m
        l_ref[g, :] = jnp.sum(p, axis=-1, keepdims=True)
        acc_ref[g, :] = jnp.dot(p.astype(BF16), v, preferred_element_type=F32)

    def update(g, a, v):
        m = m_ref[g, :]
        m_new = jnp.maximum(m, jnp.max(a, axis=-1, keepdims=True))
        alpha = jnp.exp(m - m_new)
        p = jnp.exp(a - m_new)
        m_ref[g, :] = m_new
        l_ref[g, :] = alpha * l_ref[g, :] + jnp.sum(p, axis=-1, keepdims=True)
        acc_ref[g, :] = alpha * acc_ref[g, :] + jnp.dot(p.astype(BF16), v,
                                                        preferred_element_type=F32)

    def update_fixed_max(g, a, v):
        p = jnp.exp(a - m_ref[g, :])
        l_ref[g, :] += jnp.sum(p, axis=-1, keepdims=True)
        acc_ref[g, :] += jnp.dot(p.astype(BF16), v, preferred_element_type=F32)

    def block(j, step):
        start = pl.multiple_of(j * t, t)
        k = k_ref[0, pl.ds(start, t), :]
        v = v_ref[0, pl.ds(start, t), :]
        bias = nb_ref[0, 0, pl.ds(j, 1), :]
        for g, qp in zip(groups, q_parts):
            step(g, _qk(qp, k) + bias, v)

    start = pl.multiple_of(i * t, t)
    for r, (g, qp) in enumerate(zip(groups, q_parts)):
        width = (r + 1) * hs
        row, col = _tile_ids(hs, width)
        a = _qk(qp, k_ref[0, pl.ds(start, width), :]) + nb_ref[0, 0, pl.ds(i, 1), 0:width]
        first_update(g, jnp.where(col <= row + r * hs, a, NEG_BIG),
                     v_ref[0, pl.ds(start, width), :])

    def body(state):
        j, _, gap = state
        fixed = gap <= FIXED_MAX_GAP

        @pl.when(fixed)
        def _():
            block(j, update_fixed_max)

        @pl.when(jnp.logical_not(fixed))
        def _():
            block(j, update)

        gap = block_gap(j - 1)
        return j - 1, reachable(j - 1, gap), gap

    gap = block_gap(i - 1)
    lax.while_loop(lambda s: s[1] != 0, body, (i - 1, reachable(i - 1, gap), gap))
    if has_prefix:
        @pl.when(logit_gap(pkn_ref[h], 0.0) >= -EXIT_THRESHOLD)
        def _():
            k, v, bias = kp_ref[...], vp_ref[...], pb_ref[pl.ds(h, 1), :]
            for g, qp in zip(groups, q_parts):
                update(g, _qk(qp, k) + bias, v)
    o_ref[0] = acc_ref[...] / l_ref[...]


def _sb_kernel(*refs, tq, tk, single_tile, has_prefix):
    if has_prefix:
        q_ref, k_ref, v_ref, tri_ref, kp_ref, vp_ref, trip_ref, o_ref, run_ref = refs
    else:
        q_ref, k_ref, v_ref, tri_ref, o_ref, run_ref = refs
    i = pl.program_id(2)
    n_split = tq // tk
    first = i * n_split
    groups = [slice(r * tk, (r + 1) * tk) for r in range(n_split)]
    q_parts = [q_ref[0, g, :] for g in groups]

    def update(g, z, v, tri, valid, fresh=False):
        run = jnp.zeros((tk, 1), F32) if fresh else run_ref[g, :]
        chunk = tri.shape[0]
        log_beta = _log_sigmoid(z)
        log_1m = log_beta - z
        if valid is not None:
            log_1m = jnp.where(valid, log_1m, 0.0)
        rests = []
        for c in reversed(range(z.shape[1] // chunk)):
            x = log_1m[:, c * chunk:(c + 1) * chunk]
            hi = x.astype(BF16)
            lo = (x - hi.astype(F32)).astype(BF16)
            rests.append(jnp.dot(hi, tri, preferred_element_type=F32)
                         + jnp.dot(lo, tri, preferred_element_type=F32) + run)
            run = run + jnp.sum(x, axis=-1, keepdims=True)
        rest = rests[0] if len(rests) == 1 else jnp.concatenate(rests[::-1], axis=1)
        a = jnp.exp(log_beta + rest)
        if valid is not None:
            a = jnp.where(valid, a, 0.0)
        pv = jnp.dot(a.astype(BF16), v, preferred_element_type=F32)
        o_ref[0, g, :] = pv if fresh else o_ref[0, g, :] + pv
        run_ref[g, :] = run

    def reachable(g):
        return jnp.max(run_ref[g, :]) >= -EXIT_THRESHOLD

    def own_block(g, qp, j):
        start = pl.multiple_of(j * tk, tk)
        row, col = _tile_ids(tk, tk)
        update(g, _qk(qp, k_ref[0, pl.ds(start, tk), :]), v_ref[0, pl.ds(start, tk), :],
               tri_ref[...], col < row, fresh=True)

    def own_and_previous_block(g, qp, j):
        start = pl.multiple_of((j - 1) * tk, tk)
        row, col = _tile_ids(tk, 2 * tk)
        update(g, _qk(qp, k_ref[0, pl.ds(start, 2 * tk), :]),
               v_ref[0, pl.ds(start, 2 * tk), :], tri_ref[...], col < row + tk, fresh=True)

    def spans(with_previous):
        for r, (g, qp) in enumerate(zip(groups, q_parts)):
            if r > 0 or with_previous:
                own_and_previous_block(g, qp, first + r)
            else:
                own_block(g, qp, first)

    if single_tile:
        spans(False)
    else:
        pl.when(i > 0)(lambda: spans(True))
        pl.when(i == 0)(lambda: spans(False))

    def walk_flags(n):
        return [jnp.logical_and(first + r - 2 - n >= 0, reachable(g))
                for r, g in enumerate(groups)]

    def any_flag(flags):
        return functools.reduce(jnp.logical_or, flags).astype(jnp.int32)

    def body(state):
        n, _ = state
        for r, (g, qp, go) in enumerate(zip(groups, q_parts, walk_flags(n))):
            @pl.when(go)
            def _(g=g, qp=qp, j=first + r - 2 - n):
                start = pl.multiple_of(jnp.maximum(j, 0) * tk, tk)
                update(g, _qk(qp, k_ref[0, pl.ds(start, tk), :]),
                       v_ref[0, pl.ds(start, tk), :], tri_ref[...], None)
        return n + 1, any_flag(walk_flags(n + 1))

    lax.while_loop(lambda s: s[1] != 0, body, (jnp.int32(0), any_flag(walk_flags(0))))
    if has_prefix:
        @pl.when(any_flag([reachable(g) for g in groups]) != 0)
        def _():
            _, col = _tile_ids(tk, META_ROWS)
            for g, qp in zip(groups, q_parts):
                update(g, _qk(qp, kp_ref[...]), vp_ref[...], trip_ref[...], col < N_META)


def _strict_lower(n):
    return jnp.tril(jnp.ones((n, n), BF16), k=-1)


def _attention_specs(batch, seq, t):
    q_spec = pl.BlockSpec((1, t, HEAD_DIM), lambda b, h, i: (b, i, h))
    kv_spec = pl.BlockSpec((1, seq, HEAD_DIM), lambda b, h, i: (b, 0, h))
    o_spec = pl.BlockSpec((1, t, HEAD_DIM), lambda b, h, i: (b, i, h))
    out_shape = jax.ShapeDtypeStruct((batch, seq, D_GROUP), F32)
    params = pltpu.CompilerParams(
        dimension_semantics=("arbitrary", "arbitrary", "arbitrary"),
        vmem_limit_bytes=VMEM_LIMIT)
    return q_spec, kv_spec, o_spec, out_shape, params


def _prefix_spec():
    return pl.BlockSpec((META_ROWS, HEAD_DIM), lambda b, h, i: (0, h))


def _fox_attention(q, k, v, nb, key_norm, prefix, *, t):
    batch, seq, _ = q.shape
    q_spec, kv_spec, o_spec, out_shape, params = _attention_specs(batch, seq, t)
    nb4 = nb.reshape(8, batch, seq // t, t)
    nb_end = jnp.transpose(nb4[:N_HEADS, :, :, t - 1], (1, 0, 2))
    smem = pl.BlockSpec(memory_space=pltpu.SMEM)
    in_specs = [q_spec, kv_spec, kv_spec,
                pl.BlockSpec((1, 1, seq // t, t), lambda b, h, i: (h, b, 0, 0)), smem, smem]
    args = [q, k, v, nb4, key_norm.reshape(-1), nb_end.reshape(-1)]
    if prefix is not None:
        in_specs += [_prefix_spec(), _prefix_spec(), _resident((8, META_ROWS)), smem]
        args += list(prefix)
    return pl.pallas_call(
        functools.partial(_fox_kernel, t=t, has_prefix=prefix is not None),
        grid=(batch, N_HEADS, seq // t),
        in_specs=in_specs, out_specs=o_spec, out_shape=out_shape,
        scratch_shapes=[pltpu.VMEM((t, 1), F32), pltpu.VMEM((t, 1), F32),
                        pltpu.VMEM((t, HEAD_DIM), F32)],
        compiler_params=params, name="fox_attention",
    )(*args)


def _sb_attention(q, k, v, prefix, *, tq, tk):
    batch, seq, _ = q.shape
    q_spec, kv_spec, o_spec, out_shape, params = _attention_specs(batch, seq, tq)
    in_specs = [q_spec, kv_spec, kv_spec, _resident((tk, tk))]
    args = [q, k, v, _strict_lower(tk)]
    if prefix is not None:
        in_specs += [_prefix_spec(), _prefix_spec(), _resident((META_ROWS, META_ROWS))]
        args += list(prefix) + [_strict_lower(META_ROWS)]
    return pl.pallas_call(
        functools.partial(_sb_kernel, tq=tq, tk=tk, single_tile=seq == tq,
                          has_prefix=prefix is not None),
        grid=(batch, N_HEADS, seq // tq),
        in_specs=in_specs, out_specs=o_spec, out_shape=out_shape,
        scratch_shapes=[pltpu.VMEM((tq, 1), F32)],
        compiler_params=params, name="sb_attention",
    )(*args)


def _out_proj_kernel(of_ref, os_ref, x_ref, g0_ref, b0_ref, gf_ref, gs_ref, wo_ref,
                     g1_ref, b1_ref, h_ref):
    mf = _rms_norm(of_ref[...], gf_ref[...]).astype(BF16)
    ms = _rms_norm(os_ref[...], gs_ref[...]).astype(BF16)
    y = (jnp.dot(mf, wo_ref[0:D_GROUP, :], preferred_element_type=F32)
         + jnp.dot(ms, wo_ref[D_GROUP:, :], preferred_element_type=F32))
    h0 = _layer_norm(x_ref[...], g0_ref[...], b0_ref[...])
    h_ref[...] = _layer_norm(DEEPNORM_ALPHA * h0 + y, g1_ref[...], b1_ref[...])


def _out_proj(o_f, o_s, x2d, ln0_g, ln0_b, g_fox, g_sb, w_o, ln1_g, ln1_b, *, tm):
    rows, d = x2d.shape
    row_spec = lambda width: pl.BlockSpec((tm, width), lambda i: (i, 0))
    return pl.pallas_call(
        _out_proj_kernel,
        grid=(rows // tm,),
        in_specs=[row_spec(D_GROUP), row_spec(D_GROUP), row_spec(d),
                  _resident((1, d)), _resident((1, d)),
                  _resident((1, D_GROUP)), _resident((1, D_GROUP)),
                  _resident(w_o.shape), _resident((1, d)), _resident((1, d))],
        out_specs=row_spec(d),
        out_shape=jax.ShapeDtypeStruct((rows, d), F32),
        compiler_params=pltpu.CompilerParams(
            dimension_semantics=("arbitrary",), vmem_limit_bytes=VMEM_LIMIT),
        name="out_proj",
    )(o_f, o_s, x2d, ln0_g, ln0_b, g_fox, g_sb, w_o, ln1_g, ln1_b)


def _gate_kernel(h_ref, w_ref, g_ref):
    g_ref[...] = jnp.dot(h_ref[...].astype(BF16), w_ref[...], preferred_element_type=F32)


def _gate_rows(h_rows, w_gu, d_ff, *, tf):
    rows, d = h_rows.shape
    return pl.pallas_call(
        _gate_kernel,
        grid=(d_ff // tf,),
        in_specs=[_resident((rows, d)), pl.BlockSpec((d, tf), lambda f: (0, f))],
        out_specs=pl.BlockSpec((rows, tf), lambda f: (0, f)),
        out_shape=jax.ShapeDtypeStruct((rows, d_ff), F32),
        compiler_params=pltpu.CompilerParams(dimension_semantics=("arbitrary",)),
        name="gate_rows",
    )(h_rows, w_gu)


def _ffn_kernel(h_ref, wg_ref, wu_ref, wd_ref, cw_ref, cb_ref, gi_ref, g2_ref, b2_ref,
                o_ref, hb_ref, acc_ref, hist_ref, *, tm):
    i = pl.program_id(1)
    f = pl.program_id(2)

    @pl.when(f == 0)
    def _():
        hb_ref[...] = h_ref[0].astype(BF16)
        acc_ref[...] = jnp.zeros_like(acc_ref)

    @pl.when(i == 0)
    def _():
        hist_ref[f] = gi_ref[...]

    hb = hb_ref[...]
    g = jnp.dot(hb, wg_ref[...], preferred_element_type=F32)
    u = jnp.dot(hb, wu_ref[...], preferred_element_type=F32)
    hist = hist_ref[f]
    prev1 = hist[7:8, :]
    prev2 = hist[6:7, :]
    hist_ref[f] = g[tm - 8:, :]
    row = lax.broadcasted_iota(jnp.int32, g.shape, 0)
    g1 = jnp.where(row == 0, prev1, pltpu.roll(g, 1, 0))
    g2 = jnp.where(row == 0, prev2, jnp.where(row == 1, prev1, pltpu.roll(g, 2, 0)))
    cw = cw_ref[...]
    gc = cw[0:1, :] * g2 + cw[1:2, :] * g1 + cw[2:3, :] * g + cb_ref[...]
    act = gc * (1.0 / (1.0 + jnp.exp(-gc))) * u
    acc_ref[...] += jnp.dot(act.astype(BF16), wd_ref[...], preferred_element_type=F32)

    @pl.when(f == pl.num_programs(2) - 1)
    def _():
        o_ref[0] = _layer_norm(DEEPNORM_ALPHA * h_ref[0] + acc_ref[...],
                               g2_ref[...], b2_ref[...])


def _ffn(h1, w_gu, w_down, conv_w, conv_b, g_init, ln2_g, ln2_b, *, tm, tf):
    batch, seq, d = h1.shape
    d_ff = w_down.shape[0]
    n_f = d_ff // tf
    return pl.pallas_call(
        functools.partial(_ffn_kernel, tm=tm),
        grid=(batch, seq // tm, n_f),
        in_specs=[
            pl.BlockSpec((1, tm, d), lambda b, i, f: (b, i, 0)),
            pl.BlockSpec((d, tf), lambda b, i, f: (0, f)),
            pl.BlockSpec((d, tf), lambda b, i, f: (0, n_f + f)),
            pl.BlockSpec((tf, d), lambda b, i, f: (f, 0)),
            pl.BlockSpec((CONV_WIDTH, tf), lambda b, i, f: (0, f)),
            pl.BlockSpec((1, tf), lambda b, i, f: (0, f)),
            pl.BlockSpec((8, tf), lambda b, i, f: (0, f)),
            _resident((1, d)), _resident((1, d)),
        ],
        out_specs=pl.BlockSpec((1, tm, d), lambda b, i, f: (b, i, 0)),
        out_shape=jax.ShapeDtypeStruct((batch, seq, d), F32),
        scratch_shapes=[pltpu.VMEM((tm, d), BF16), pltpu.VMEM((tm, d), F32),
                        pltpu.VMEM((n_f, 8, tf), F32)],
        compiler_params=pltpu.CompilerParams(
            dimension_semantics=("arbitrary", "arbitrary", "arbitrary"),
            vmem_limit_bytes=VMEM_LIMIT),
        name="conv_ffn",
    )(h1, w_gu, w_gu, w_down, conv_w, conv_b, g_init, ln2_g, ln2_b)


def _pick(n, preferred):
    return preferred if n % preferred == 0 else n


def kernel(x, meta, ln0_g, ln0_b, w_in, b_f, g_fox, g_sb, w_o, ln1_g, ln1_b, w_gu, conv_w,
           conv_b, w_down, ln2_g, ln2_b):
    batch, seq, d = x.shape
    assert w_in.shape[0] == 1, "single layer"
    assert d == 2 * D_GROUP
    rows = batch * seq
    d_ff = w_down.shape[1]
    tm = _pick(seq, 512)
    tq_sb = _pick(seq, 1024)
    tk_sb = _pick(seq, 256)
    tm_ffn = _pick(seq, 512)
    tf = _pick(d_ff, 512)

    row = lambda a: a.reshape(1, -1).astype(F32)
    w = w_in[0]
    w_fox = w[:, :3 * D_GROUP].astype(BF16)
    w_sb = w[:, 3 * D_GROUP + N_HEADS:].astype(BF16)
    w_gate = jnp.pad(w[:, 3 * D_GROUP:3 * D_GROUP + N_HEADS],
                     ((0, 0), (0, GATE_COLS - N_HEADS))).astype(BF16)
    bf_row = jnp.pad(b_f[0].astype(F32), (0, GATE_COLS - N_HEADS)).reshape(1, GATE_COLS)
    w_o_b = w_o[0].astype(BF16)
    w_gu_b = w_gu[0].astype(BF16)
    w_down_b = w_down[0].astype(BF16)
    ln0 = (row(ln0_g), row(ln0_b))
    ln1 = (row(ln1_g[0]), row(ln1_b[0]))
    ln2 = (row(ln2_g[0]), row(ln2_b[0]))
    gf, gs = row(g_fox[0]), row(g_sb[0])

    def mixer(x2d, seqs, seq_len, t, tq_sb, tk, meta_keys):
        qf, kf, vf, qs, ks, vs, nb, kn = _in_proj(x2d, *ln0, w_fox, w_sb, w_gate, bf_row,
                                                  rows_per_seq=seq_len, tm=t)
        kn = jnp.sqrt(kn[:, :N_HEADS, 0]).reshape(seqs, seq_len // t, N_HEADS)
        kn = lax.cummax(jnp.transpose(kn, (0, 2, 1)), axis=2)
        prefix_f = prefix_s = None
        if meta_keys is not None:
            kf_m, vf_m, nb_m, kn_m, ks_m, vs_m = meta_keys
            kn = jnp.maximum(kn, kn_m[0, :, :1])
            valid = jnp.arange(META_ROWS)[None, :] < N_META
            pbias = jnp.where(valid, nb_m - nb_m[:, N_META - 1:N_META], NEG_BIG)
            prefix_f = (kf_m, vf_m, pbias, kn_m.reshape(-1))
            prefix_s = (ks_m, vs_m)
        shape3 = (seqs, seq_len, D_GROUP)
        o_f = _fox_attention(qf.reshape(shape3), kf.reshape(shape3), vf.reshape(shape3),
                             nb, kn, prefix_f, t=t)
        o_s = _sb_attention(qs.reshape(shape3), ks.reshape(shape3), vs.reshape(shape3),
                            prefix_s, tq=tq_sb, tk=tk)
        h1 = _out_proj(o_f.reshape(-1, D_GROUP), o_s.reshape(-1, D_GROUP), x2d, *ln0,
                       gf, gs, w_o_b, *ln1, tm=t)
        return h1, (kf, vf, nb, kn, ks, vs)

    meta_pad = jnp.zeros((META_ROWS, d), F32).at[:N_META].set(meta.astype(F32))
    h1_meta, meta_keys = mixer(meta_pad, 1, META_ROWS, META_ROWS, META_ROWS, META_ROWS, None)
    g_meta = _gate_rows(h1_meta[:N_META], w_gu_b, d_ff, tf=tf)
    g_init = g_meta[N_META - 8:N_META]

    h1, _ = mixer(x.reshape(rows, d).astype(F32), batch, seq, tm, tq_sb, tk_sb, meta_keys)
    out = _ffn(h1.reshape(batch, seq, d), w_gu_b, w_down_b, conv_w[0].astype(F32),
               row(conv_b[0]), g_init, *ln2, tm=tm_ffn, tf=tf)
    return out.astype(x.dtype)
```

```python
import functools
import math

import jax
import jax.numpy as jnp
from jax import lax
from jax.experimental import pallas as pl
from jax.experimental.pallas import tpu as pltpu

N_META = 16
HEAD_DIM = 256
N_HEADS = 4
D_GROUP = N_HEADS * HEAD_DIM
CONV_WIDTH = 3
LN_EPS = 1e-5
RMS_EPS = 1e-6
DEEPNORM_ALPHA = 2.0 ** 0.25
GATE_COLS = 128
META_ROWS = 128
NEG_BIG = -1e30
EXIT_THRESHOLD = 110.0
FIXED_MAX_GAP = 40.0

F32 = jnp.float32
BF16 = jnp.bfloat16

VMEM_LIMIT = 56 * 1024 * 1024


def _layer_norm(x, g, b):
    mu = jnp.mean(x, axis=-1, keepdims=True)
    xc = x - mu
    var = jnp.mean(xc * xc, axis=-1, keepdims=True)
    return xc * lax.rsqrt(var + LN_EPS) * g + b


def _rms_norm(x, g):
    ms = jnp.mean(x * x, axis=-1, keepdims=True)
    return x * lax.rsqrt(ms + RMS_EPS) * g


def _log_sigmoid(z):
    return jnp.minimum(z, 0.0) - jnp.log(1.0 + jnp.exp(-jnp.abs(z)))


def _split3(x):
    hi = x.astype(BF16)
    r1 = x - hi.astype(F32)
    mid = r1.astype(BF16)
    lo = (r1 - mid.astype(F32)).astype(BF16)
    return hi, mid, lo


def _resident(shape):
    nd = len(shape)
    return pl.BlockSpec(shape, lambda *_: (0,) * nd, pipeline_mode=pl.Buffered(1))


def _split_w_in_kernel(w_ref, wf_ref, ws_ref, wg_ref):
    w = w_ref[0]
    wf_ref[...] = w[:, :3 * D_GROUP].astype(BF16)
    ws_ref[...] = w[:, 3 * D_GROUP + N_HEADS:].astype(BF16)
    gate = w[:, 3 * D_GROUP:3 * D_GROUP + GATE_COLS]
    lane = lax.broadcasted_iota(jnp.int32, gate.shape, 1)
    wg_ref[...] = jnp.where(lane < N_HEADS, gate, 0.0).astype(BF16)


def _split_w_in(w_in, *, tr):
    _, d, cols = w_in.shape
    out_spec = lambda width: pl.BlockSpec((tr, width), lambda i: (i, 0))
    return pl.pallas_call(
        _split_w_in_kernel,
        grid=(d // tr,),
        in_specs=[pl.BlockSpec((1, tr, cols), lambda i: (0, i, 0))],
        out_specs=[out_spec(3 * D_GROUP), out_spec(3 * D_GROUP), out_spec(GATE_COLS)],
        out_shape=[jax.ShapeDtypeStruct((d, 3 * D_GROUP), BF16),
                   jax.ShapeDtypeStruct((d, 3 * D_GROUP), BF16),
                   jax.ShapeDtypeStruct((d, GATE_COLS), BF16)],
        compiler_params=pltpu.CompilerParams(
            dimension_semantics=("arbitrary",), vmem_limit_bytes=VMEM_LIMIT),
        name="split_w_in",
    )(w_in)


def _max_head_norms(yb):
    y = yb.astype(F32)
    sq = y * y
    rows = []
    for h in range(N_HEADS):
        n2 = jnp.sum(sq[:, h * HEAD_DIM:(h + 1) * HEAD_DIM], axis=-1, keepdims=True)
        rows.append(jnp.broadcast_to(jnp.max(n2, axis=0, keepdims=True), (1, GATE_COLS)))
    rows.append(jnp.zeros((8 - N_HEADS, GATE_COLS), F32))
    return jnp.concatenate(rows, axis=0)


def _in_proj_kernel(x_ref, g_ref, b_ref, wf_ref, ws_ref, wg_ref, bf_ref, tril_ref,
                    qf_ref, kf_ref, vf_ref, qs_ref, ks_ref, vs_ref, nb_ref, kn_ref,
                    carry_ref, *, tiles_per_seq):
    i = pl.program_id(0)
    scale = 1.0 / math.sqrt(HEAD_DIM)
    hn = _layer_norm(x_ref[...], g_ref[...], b_ref[...])
    hb = hn.astype(BF16)
    outs = ((qf_ref, kf_ref, vf_ref), (qs_ref, ks_ref, vs_ref))
    for w_ref, group in zip((wf_ref, ws_ref), outs):
        for n, o_ref in enumerate(group):
            y = jnp.dot(hb, w_ref[:, n * D_GROUP:(n + 1) * D_GROUP],
                        preferred_element_type=F32)
            if n == 0:
                y = y * scale
            yb = y.astype(BF16)
            o_ref[...] = yb
            if o_ref is kf_ref:
                kn_ref[0] = _max_head_norms(yb)

    f_logit = jnp.dot(hb, wg_ref[...], preferred_element_type=F32)
    log_f = _log_sigmoid(f_logit + bf_ref[...])

    @pl.when(i % tiles_per_seq == 0)
    def _():
        carry_ref[...] = jnp.zeros_like(carry_ref)

    tril = tril_ref[...]
    hi, mid, lo = _split3(log_f)
    c = (jnp.dot(tril, hi, preferred_element_type=F32)
         + jnp.dot(tril, mid, preferred_element_type=F32)
         + jnp.dot(tril, lo, preferred_element_type=F32))
    c = c + carry_ref[0:1, :]
    carry_ref[...] = jnp.broadcast_to(c[-1:, :], carry_ref.shape)
    nb_ref[...] = -jnp.transpose(c)[0:8, :]


def _in_proj(x2d, ln_g, ln_b, w_fox, w_sb, w_gate, bf_row, *, rows_per_seq, tm):
    rows, d = x2d.shape
    n_tiles = rows // tm
    tril = jnp.tril(jnp.ones((tm, tm), BF16))
    qkv_shape = jax.ShapeDtypeStruct((rows, D_GROUP), BF16)
    qkv_spec = pl.BlockSpec((tm, D_GROUP), lambda i: (i, 0))
    return pl.pallas_call(
        functools.partial(_in_proj_kernel, tiles_per_seq=rows_per_seq // tm),
        grid=(n_tiles,),
        in_specs=[
            pl.BlockSpec((tm, d), lambda i: (i, 0)),
            _resident((1, d)), _resident((1, d)),
            _resident(w_fox.shape), _resident(w_sb.shape), _resident(w_gate.shape),
            _resident((1, GATE_COLS)),
            _resident((tm, tm)),
        ],
        out_specs=[qkv_spec] * 6 + [pl.BlockSpec((8, tm), lambda i: (0, i)),
                                    pl.BlockSpec((1, 8, GATE_COLS), lambda i: (i, 0, 0))],
        out_shape=[qkv_shape] * 6 + [jax.ShapeDtypeStruct((8, rows), F32),
                                     jax.ShapeDtypeStruct((n_tiles, 8, GATE_COLS), F32)],
        scratch_shapes=[pltpu.VMEM((8, GATE_COLS), F32)],
        compiler_params=pltpu.CompilerParams(
            dimension_semantics=("arbitrary",), vmem_limit_bytes=VMEM_LIMIT),
        name="in_proj",
    )(x2d, ln_g, ln_b, w_fox, w_sb, w_gate, bf_row, tril)


def _tile_ids(tq, tk):
    row = lax.broadcasted_iota(jnp.int32, (tq, tk), 0)
    col = lax.broadcasted_iota(jnp.int32, (tq, tk), 1)
    return row, col


def _row_groups(t):
    return 2 if t % 512 == 0 else 1


def _qk(q, k):
    return lax.dot_general(q, k, (((1,), (1,)), ((), ())), preferred_element_type=F32)


def _fox_kernel(*refs, t, has_prefix):
    if has_prefix:
        (q_ref, k_ref, v_ref, nb_ref, kn_ref, nbe_ref, kp_ref, vp_ref, pb_ref, pkn_ref,
         o_ref, m_ref, l_ref, acc_ref) = refs
    else:
        q_ref, k_ref, v_ref, nb_ref, kn_ref, nbe_ref, o_ref, m_ref, l_ref, acc_ref = refs
    h = pl.program_id(1)
    i = pl.program_id(2)
    base = (pl.program_id(0) * N_HEADS + h) * pl.num_programs(2)
    n_split = _row_groups(t)
    hs = t // n_split
    groups = [slice(r * hs, (r + 1) * hs) for r in range(n_split)]
    q_parts = [q_ref[0, g, :] for g in groups]

    def max_norm(qp):
        qf = qp.astype(F32)
        return jnp.max(jnp.sqrt(jnp.sum(qf * qf, axis=-1, keepdims=True)))

    q_norm = functools.reduce(jnp.maximum, [max_norm(qp) for qp in q_parts])

    def logit_gap(key_norm, bias):
        return q_norm * key_norm + bias - jnp.min(m_ref[...])

    def block_gap(j):
        jc = jnp.maximum(j, 0)
        return logit_gap(kn_ref[base + jc], nbe_ref[base + jc])

    def reachable(j, gap):
        return jnp.logical_and(j >= 0, gap >= -EXIT_THRESHOLD).astype(jnp.int32)

    def first_update(g, a, v):
        m = jnp.max(a, axis=-1, keepdims=True)
        p = jnp.exp(a - m)
        m_ref[g, :] = m
        l_ref[g, :] = jnp.sum(p, axis=-1, keepdims=True)
        acc_ref[g, :] = jnp.dot(p.astype(BF16), v, preferred_element_type=F32)

    def update(g, a, v):
        m = m_ref[g, :]
        m_new = jnp.maximum(m, jnp.max(a, axis=-1, keepdims=True))
        alpha = jnp.exp(m - m_new)
        p = jnp.exp(a - m_new)
        m_ref[g, :] = m_new
        l_ref[g, :] = alpha * l_ref[g, :] + jnp.sum(p, axis=-1, keepdims=True)
        acc_ref[g, :] = alpha * acc_ref[g, :] + jnp.dot(p.astype(BF16), v,
                                                        preferred_element_type=F32)

    def update_fixed_max(g, a, v):
        p = jnp.exp(a - m_ref[g, :])
        l_ref[g, :] += jnp.sum(p, axis=-1, keepdims=True)
        acc_ref[g, :] += jnp.dot(p.astype(BF16), v, preferred_element_type=F32)

    def block(j, step):
        start = pl.multiple_of(j * t, t)
        k = k_ref[0, pl.ds(start, t), :]
        v = v_ref[0, pl.ds(start, t), :]
        bias = nb_ref[0, 0, pl.ds(j, 1), :]
        for g, qp in zip(groups, q_parts):
            step(g, _qk(qp, k) + bias, v)

    start = pl.multiple_of(i * t, t)
    for r, (g, qp) in enumerate(zip(groups, q_parts)):
        width = (r + 1) * hs
        row, col = _tile_ids(hs, width)
        a = _qk(qp, k_ref[0, pl.ds(start, width), :]) + nb_ref[0, 0, pl.ds(i, 1), 0:width]
        first_update(g, jnp.where(col <= row + r * hs, a, NEG_BIG),
                     v_ref[0, pl.ds(start, width), :])

    def body(state):
        j, _, gap = state
        fixed = gap <= FIXED_MAX_GAP

        @pl.when(fixed)
        def _():
            block(j, update_fixed_max)

        @pl.when(jnp.logical_not(fixed))
        def _():
            block(j, update)

        gap = block_gap(j - 1)
        return j - 1, reachable(j - 1, gap), gap

    gap = block_gap(i - 1)
    lax.while_loop(lambda s: s[1] != 0, body, (i - 1, reachable(i - 1, gap), gap))
    if has_prefix:
        @pl.when(logit_gap(pkn_ref[h], 0.0) >= -EXIT_THRESHOLD)
        def _():
            k, v, bias = kp_ref[...], vp_ref[...], pb_ref[pl.ds(h, 1), :]
            for g, qp in zip(groups, q_parts):
                update(g, _qk(qp, k) + bias, v)
    o_ref[0] = acc_ref[...] / l_ref[...]


def _sb_kernel(*refs, tq, tk, single_tile, has_prefix):
    if has_prefix:
        q_ref, k_ref, v_ref, tri_ref, kp_ref, vp_ref, trip_ref, o_ref, run_ref = refs
    else:
        q_ref, k_ref, v_ref, tri_ref, o_ref, run_ref = refs
    i = pl.program_id(2)
    n_split = tq // tk
    first = i * n_split
    groups = [slice(r * tk, (r + 1) * tk) for r in range(n_split)]
    q_parts = [q_ref[0, g, :] for g in groups]

    def update(g, z, v, tri, valid, fresh=False):
        run = jnp.zeros((tk, 1), F32) if fresh else run_ref[g, :]
        chunk = tri.shape[0]
        log_beta = _log_sigmoid(z)
        log_1m = log_beta - z
        if valid is not None:
            log_1m = jnp.where(valid, log_1m, 0.0)
        rests = []
        for c in reversed(range(z.shape[1] // chunk)):
            x = log_1m[:, c * chunk:(c + 1) * chunk]
            hi = x.astype(BF16)
            lo = (x - hi.astype(F32)).astype(BF16)
            rests.append(jnp.dot(hi, tri, preferred_element_type=F32)
                         + jnp.dot(lo, tri, preferred_element_type=F32) + run)
            run = run + jnp.sum(x, axis=-1, keepdims=True)
        rest = rests[0] if len(rests) == 1 else jnp.concatenate(rests[::-1], axis=1)
        a = jnp.exp(log_beta + rest)
        if valid is not None:
            a = jnp.where(valid, a, 0.0)
        pv = jnp.dot(a.astype(BF16), v, preferred_element_type=F32)
        o_ref[0, g, :] = pv if fresh else o_ref[0, g, :] + pv
        run_ref[g, :] = run

    def reachable(g):
        return jnp.max(run_ref[g, :]) >= -EXIT_THRESHOLD

    def own_block(g, qp, j):
        start = pl.multiple_of(j * tk, tk)
        row, col = _tile_ids(tk, tk)
        update(g, _qk(qp, k_ref[0, pl.ds(start, tk), :]), v_ref[0, pl.ds(start, tk), :],
               tri_ref[...], col < row, fresh=True)

    def own_and_previous_block(g, qp, j):
        start = pl.multiple_of((j - 1) * tk, tk)
        row, col = _tile_ids(tk, 2 * tk)
        update(g, _qk(qp, k_ref[0, pl.ds(start, 2 * tk), :]),
               v_ref[0, pl.ds(start, 2 * tk), :], tri_ref[...], col < row + tk, fresh=True)

    def spans(with_previous):
        for r, (g, qp) in enumerate(zip(groups, q_parts)):
            if r > 0 or with_previous:
                own_and_previous_block(g, qp, first + r)
            else:
                own_block(g, qp, first)

    if single_tile:
        spans(False)
    else:
        pl.when(i > 0)(lambda: spans(True))
        pl.when(i == 0)(lambda: spans(False))

    def walk_flags(n):
        return [jnp.logical_and(first + r - 2 - n >= 0, reachable(g))
                for r, g in enumerate(groups)]

    def any_flag(flags):
        return functools.reduce(jnp.logical_or, flags).astype(jnp.int32)

    def body(state):
        n, _ = state
        for r, (g, qp, go) in enumerate(zip(groups, q_parts, walk_flags(n))):
            @pl.when(go)
            def _(g=g, qp=qp, j=first + r - 2 - n):
                start = pl.multiple_of(jnp.maximum(j, 0) * tk, tk)
                update(g, _qk(qp, k_ref[0, pl.ds(start, tk), :]),
                       v_ref[0, pl.ds(start, tk), :], tri_ref[...], None)
        return n + 1, any_flag(walk_flags(n + 1))

    lax.while_loop(lambda s: s[1] != 0, body, (jnp.int32(0), any_flag(walk_flags(0))))
    if has_prefix:
        @pl.when(any_flag([reachable(g) for g in groups]) != 0)
        def _():
            _, col = _tile_ids(tk, META_ROWS)
            for g, qp in zip(groups, q_parts):
                update(g, _qk(qp, kp_ref[...]), vp_ref[...], trip_ref[...], col < N_META)


def _strict_lower(n):
    return jnp.tril(jnp.ones((n, n), BF16), k=-1)


def _attention_specs(batch, seq, t):
    q_spec = pl.BlockSpec((1, t, HEAD_DIM), lambda b, h, i: (b, i, h))
    kv_spec = pl.BlockSpec((1, seq, HEAD_DIM), lambda b, h, i: (b, 0, h))
    o_spec = pl.BlockSpec((1, t, HEAD_DIM), lambda b, h, i: (b, i, h))
    out_shape = jax.ShapeDtypeStruct((batch, seq, D_GROUP), F32)
    params = pltpu.CompilerParams(
        dimension_semantics=("arbitrary", "arbitrary", "arbitrary"),
        vmem_limit_bytes=VMEM_LIMIT)
    return q_spec, kv_spec, o_spec, out_shape, params


def _prefix_spec():
    return pl.BlockSpec((META_ROWS, HEAD_DIM), lambda b, h, i: (0, h))


def _fox_attention(q, k, v, nb, key_norm, prefix, *, t):
    batch, seq, _ = q.shape
    q_spec, kv_spec, o_spec, out_shape, params = _attention_specs(batch, seq, t)
    nb4 = nb.reshape(8, batch, seq // t, t)
    nb_end = jnp.transpose(nb4[:N_HEADS, :, :, t - 1], (1, 0, 2))
    smem = pl.BlockSpec(memory_space=pltpu.SMEM)
    in_specs = [q_spec, kv_spec, kv_spec,
                pl.BlockSpec((1, 1, seq // t, t), lambda b, h, i: (h, b, 0, 0)), smem, smem]
    args = [q, k, v, nb4, key_norm.reshape(-1), nb_end.reshape(-1)]
    if prefix is not None:
        in_specs += [_prefix_spec(), _prefix_spec(), _resident((8, META_ROWS)), smem]
        args += list(prefix)
    return pl.pallas_call(
        functools.partial(_fox_kernel, t=t, has_prefix=prefix is not None),
        grid=(batch, N_HEADS, seq // t),
        in_specs=in_specs, out_specs=o_spec, out_shape=out_shape,
        scratch_shapes=[pltpu.VMEM((t, 1), F32), pltpu.VMEM((t, 1), F32),
                        pltpu.VMEM((t, HEAD_DIM), F32)],
        compiler_params=params, name="fox_attention",
    )(*args)


def _sb_attention(q, k, v, prefix, *, tq, tk):
    batch, seq, _ = q.shape
    q_spec, kv_spec, o_spec, out_shape, params = _attention_specs(batch, seq, tq)
    in_specs = [q_spec, kv_spec, kv_spec, _resident((tk, tk))]
    args = [q, k, v, _strict_lower(tk)]
    if prefix is not None:
        in_specs += [_prefix_spec(), _prefix_spec(), _resident((META_ROWS, META_ROWS))]
        args += list(prefix) + [_strict_lower(META_ROWS)]
    return pl.pallas_call(
        functools.partial(_sb_kernel, tq=tq, tk=tk, single_tile=seq == tq,
                          has_prefix=prefix is not None),
        grid=(batch, N_HEADS, seq // tq),
        in_specs=in_specs, out_specs=o_spec, out_shape=out_shape,
        scratch_shapes=[pltpu.VMEM((tq, 1), F32)],
        compiler_params=params, name="sb_attention",
    )(*args)


def _out_proj_kernel(of_ref, os_ref, x_ref, g0_ref, b0_ref, gf_ref, gs_ref, wo_ref,
                     g1_ref, b1_ref, h_ref):
    mf = _rms_norm(of_ref[...], gf_ref[...]).astype(BF16)
    ms = _rms_norm(os_ref[...], gs_ref[...]).astype(BF16)
    y = (jnp.dot(mf, wo_ref[0:D_GROUP, :], preferred_element_type=F32)
         + jnp.dot(ms, wo_ref[D_GROUP:, :], preferred_element_type=F32))
    h0 = _layer_norm(x_ref[...], g0_ref[...], b0_ref[...])
    h_ref[...] = _layer_norm(DEEPNORM_ALPHA * h0 + y, g1_ref[...], b1_ref[...])


def _out_proj(o_f, o_s, x2d, ln0_g, ln0_b, g_fox, g_sb, w_o, ln1_g, ln1_b, *, tm):
    rows, d = x2d.shape
    row_spec = lambda width: pl.BlockSpec((tm, width), lambda i: (i, 0))
    return pl.pallas_call(
        _out_proj_kernel,
        grid=(rows // tm,),
        in_specs=[row_spec(D_GROUP), row_spec(D_GROUP), row_spec(d),
                  _resident((1, d)), _resident((1, d)),
                  _resident((1, D_GROUP)), _resident((1, D_GROUP)),
                  _resident(w_o.shape), _resident((1, d)), _resident((1, d))],
        out_specs=row_spec(d),
        out_shape=jax.ShapeDtypeStruct((rows, d), F32),
        compiler_params=pltpu.CompilerParams(
            dimension_semantics=("arbitrary",), vmem_limit_bytes=VMEM_LIMIT),
        name="out_proj",
    )(o_f, o_s, x2d, ln0_g, ln0_b, g_fox, g_sb, w_o, ln1_g, ln1_b)


def _gate_kernel(h_ref, w_ref, g_ref):
    g_ref[...] = jnp.dot(h_ref[...].astype(BF16), w_ref[...], preferred_element_type=F32)


def _gate_rows(h_rows, w_gu, d_ff, *, tf):
    rows, d = h_rows.shape
    return pl.pallas_call(
        _gate_kernel,
        grid=(d_ff // tf,),
        in_specs=[_resident((rows, d)), pl.BlockSpec((d, tf), lambda f: (0, f))],
        out_specs=pl.BlockSpec((rows, tf), lambda f: (0, f)),
        out_shape=jax.ShapeDtypeStruct((rows, d_ff), F32),
        compiler_params=pltpu.CompilerParams(dimension_semantics=("arbitrary",)),
        name="gate_rows",
    )(h_rows, w_gu)


def _ffn_kernel(h_ref, wg_ref, wu_ref, wd_ref, cw_ref, cb_ref, gi_ref, g2_ref, b2_ref,
                o_ref, hb_ref, acc_ref, hist_ref, *, tm):
    i = pl.program_id(1)
    f = pl.program_id(2)

    @pl.when(f == 0)
    def _():
        hb_ref[...] = h_ref[0].astype(BF16)
        acc_ref[...] = jnp.zeros_like(acc_ref)

    @pl.when(i == 0)
    def _():
        hist_ref[f] = gi_ref[...]

    hb = hb_ref[...]
    g = jnp.dot(hb, wg_ref[...], preferred_element_type=F32)
    u = jnp.dot(hb, wu_ref[...], preferred_element_type=F32)
    hist = hist_ref[f]
    prev1 = hist[7:8, :]
    prev2 = hist[6:7, :]
    hist_ref[f] = g[tm - 8:, :]
    row = lax.broadcasted_iota(jnp.int32, g.shape, 0)
    g1 = jnp.where(row == 0, prev1, pltpu.roll(g, 1, 0))
    g2 = jnp.where(row == 0, prev2, jnp.where(row == 1, prev1, pltpu.roll(g, 2, 0)))
    cw = cw_ref[...]
    gc = cw[0:1, :] * g2 + cw[1:2, :] * g1 + cw[2:3, :] * g + cb_ref[...]
    act = gc * (1.0 / (1.0 + jnp.exp(-gc))) * u
    acc_ref[...] += jnp.dot(act.astype(BF16), wd_ref[...], preferred_element_type=F32)

    @pl.when(f == pl.num_programs(2) - 1)
    def _():
        o_ref[0] = _layer_norm(DEEPNORM_ALPHA * h_ref[0] + acc_ref[...],
                               g2_ref[...], b2_ref[...])


def _ffn(h1, w_gu, w_down, conv_w, conv_b, g_init, ln2_g, ln2_b, *, tm, tf):
    batch, seq, d = h1.shape
    d_ff = w_down.shape[0]
    n_f = d_ff // tf
    return pl.pallas_call(
        functools.partial(_ffn_kernel, tm=tm),
        grid=(batch, seq // tm, n_f),
        in_specs=[
            pl.BlockSpec((1, tm, d), lambda b, i, f: (b, i, 0)),
            pl.BlockSpec((d, tf), lambda b, i, f: (0, f)),
            pl.BlockSpec((d, tf), lambda b, i, f: (0, n_f + f)),
            pl.BlockSpec((tf, d), lambda b, i, f: (f, 0)),
            pl.BlockSpec((CONV_WIDTH, tf), lambda b, i, f: (0, f)),
            pl.BlockSpec((1, tf), lambda b, i, f: (0, f)),
            pl.BlockSpec((8, tf), lambda b, i, f: (0, f)),
            _resident((1, d)), _resident((1, d)),
        ],
        out_specs=pl.BlockSpec((1, tm, d), lambda b, i, f: (b, i, 0)),
        out_shape=jax.ShapeDtypeStruct((batch, seq, d), F32),
        scratch_shapes=[pltpu.VMEM((tm, d), BF16), pltpu.VMEM((tm, d), F32),
                        pltpu.VMEM((n_f, 8, tf), F32)],
        compiler_params=pltpu.CompilerParams(
            dimension_semantics=("arbitrary", "arbitrary", "arbitrary"),
            vmem_limit_bytes=VMEM_LIMIT),
        name="conv_ffn",
    )(h1, w_gu, w_gu, w_down, conv_w, conv_b, g_init, ln2_g, ln2_b)


def _pick(n, preferred):
    return preferred if n % preferred == 0 else n


def kernel(x, meta, ln0_g, ln0_b, w_in, b_f, g_fox, g_sb, w_o, ln1_g, ln1_b, w_gu, conv_w,
           conv_b, w_down, ln2_g, ln2_b):
    batch, seq, d = x.shape
    assert w_in.shape[0] == 1, "single layer"
    assert d == 2 * D_GROUP
    rows = batch * seq
    d_ff = w_down.shape[1]
    tm = _pick(seq, 512)
    tq_sb = _pick(seq, 1024)
    tk_sb = _pick(seq, 256)
    tm_ffn = _pick(seq, 512)
    tf = _pick(d_ff, 512)

    row = lambda a: a.reshape(1, -1).astype(F32)
    w_fox, w_sb, w_gate = _split_w_in(w_in.astype(F32), tr=256)
    bf_row = jnp.pad(b_f[0].astype(F32), (0, GATE_COLS - N_HEADS)).reshape(1, GATE_COLS)
    w_o_b = w_o[0].astype(BF16)
    w_gu_b = w_gu[0].astype(BF16)
    w_down_b = w_down[0].astype(BF16)
    ln0 = (row(ln0_g), row(ln0_b))
    ln1 = (row(ln1_g[0]), row(ln1_b[0]))
    ln2 = (row(ln2_g[0]), row(ln2_b[0]))
    gf, gs = row(g_fox[0]), row(g_sb[0])

    def mixer(x2d, seqs, seq_len, t, tq_sb, tk, meta_keys):
        qf, kf, vf, qs, ks, vs, nb, kn = _in_proj(x2d, *ln0, w_fox, w_sb, w_gate, bf_row,
                                                  rows_per_seq=seq_len, tm=t)
        kn = jnp.sqrt(kn[:, :N_HEADS, 0]).reshape(seqs, seq_len // t, N_HEADS)
        kn = lax.cummax(jnp.transpose(kn, (0, 2, 1)), axis=2)
        prefix_f = prefix_s = None
        if meta_keys is not None:
            kf_m, vf_m, nb_m, kn_m, ks_m, vs_m = meta_keys
            kn = jnp.maximum(kn, kn_m[0, :, :1])
            valid = jnp.arange(META_ROWS)[None, :] < N_META
            pbias = jnp.where(valid, nb_m - nb_m[:, N_META - 1:N_META], NEG_BIG)
            prefix_f = (kf_m, vf_m, pbias, kn_m.reshape(-1))
            prefix_s = (ks_m, vs_m)
        shape3 = (seqs, seq_len, D_GROUP)
        o_f = _fox_attention(qf.reshape(shape3), kf.reshape(shape3), vf.reshape(shape3),
                             nb, kn, prefix_f, t=t)
        o_s = _sb_attention(qs.reshape(shape3), ks.reshape(shape3), vs.reshape(shape3),
                            prefix_s, tq=tq_sb, tk=tk)
        h1 = _out_proj(o_f.reshape(-1, D_GROUP), o_s.reshape(-1, D_GROUP), x2d, *ln0,
                       gf, gs, w_o_b, *ln1, tm=t)
        return h1, (kf, vf, nb, kn, ks, vs)

    meta_pad = jnp.zeros((META_ROWS, d), F32).at[:N_META].set(meta.astype(F32))
    h1_meta, meta_keys = mixer(meta_pad, 1, META_ROWS, META_ROWS, META_ROWS, META_ROWS, None)
    g_meta = _gate_rows(h1_meta[:N_META], w_gu_b, d_ff, tf=tf)
    g_init = g_meta[N_META - 8:N_META]

    h1, _ = mixer(x.reshape(rows, d).astype(F32), batch, seq, tm, tq_sb, tk_sb, meta_keys)
    out = _ffn(h1.reshape(batch, seq, d), w_gu_b, w_down_b, conv_w[0].astype(F32),
               row(conv_b[0]), g_init, *ln2, tm=tm_ffn, tf=tf)
    return out.astype(x.dtype)
```

```python
import functools
import math

import jax
import jax.numpy as jnp
from jax import lax
from jax.experimental import pallas as pl
from jax.experimental.pallas import tpu as pltpu

N_META = 16
HEAD_DIM = 256
N_HEADS = 4
D_GROUP = N_HEADS * HEAD_DIM
CONV_WIDTH = 3
LN_EPS = 1e-5
RMS_EPS = 1e-6
DEEPNORM_ALPHA = 2.0 ** 0.25
GATE_COLS = 128
META_ROWS = 128
NEG_BIG = -1e30
EXIT_THRESHOLD = 110.0
FIXED_MAX_GAP = 40.0

F32 = jnp.float32
BF16 = jnp.bfloat16

VMEM_LIMIT = 56 * 1024 * 1024


def _layer_norm(x, g, b):
    mu = jnp.mean(x, axis=-1, keepdims=True)
    xc = x - mu
    var = jnp.mean(xc * xc, axis=-1, keepdims=True)
    return xc * lax.rsqrt(var + LN_EPS) * g + b


def _rms_norm(x, g):
    ms = jnp.mean(x * x, axis=-1, keepdims=True)
    return x * lax.rsqrt(ms + RMS_EPS) * g


def _log_sigmoid(z):
    return jnp.minimum(z, 0.0) - jnp.log(1.0 + jnp.exp(-jnp.abs(z)))


def _split3(x):
    hi = x.astype(BF16)
    r1 = x - hi.astype(F32)
    mid = r1.astype(BF16)
    lo = (r1 - mid.astype(F32)).astype(BF16)
    return hi, mid, lo


def _resident(shape):
    nd = len(shape)
    return pl.BlockSpec(shape, lambda *_: (0,) * nd, pipeline_mode=pl.Buffered(1))


def _split_w_in_kernel(w_ref, wf_ref, ws_ref, wg_ref):
    w = w_ref[0]
    wf_ref[...] = w[:, :3 * D_GROUP].astype(BF16)
    ws_ref[...] = w[:, 3 * D_GROUP + N_HEADS:].astype(BF16)
    gate = w[:, 3 * D_GROUP:3 * D_GROUP + GATE_COLS]
    lane = lax.broadcasted_iota(jnp.int32, gate.shape, 1)
    wg_ref[...] = jnp.where(lane < N_HEADS, gate, 0.0).astype(BF16)


def _split_w_in(w_in, *, tr):
    _, d, cols = w_in.shape
    out_spec = lambda width: pl.BlockSpec((tr, width), lambda i: (i, 0))
    return pl.pallas_call(
        _split_w_in_kernel,
        grid=(d // tr,),
        in_specs=[pl.BlockSpec((1, tr, cols), lambda i: (0, i, 0))],
        out_specs=[out_spec(3 * D_GROUP), out_spec(3 * D_GROUP), out_spec(GATE_COLS)],
        out_shape=[jax.ShapeDtypeStruct((d, 3 * D_GROUP), BF16),
                   jax.ShapeDtypeStruct((d, 3 * D_GROUP), BF16),
                   jax.ShapeDtypeStruct((d, GATE_COLS), BF16)],
        compiler_params=pltpu.CompilerParams(
            dimension_semantics=("arbitrary",), vmem_limit_bytes=VMEM_LIMIT),
        name="split_w_in",
    )(w_in)


def _max_head_norms(yb):
    y = yb.astype(F32)
    sq = y * y
    rows = []
    for h in range(N_HEADS):
        n2 = jnp.sum(sq[:, h * HEAD_DIM:(h + 1) * HEAD_DIM], axis=-1, keepdims=True)
        rows.append(jnp.broadcast_to(jnp.max(n2, axis=0, keepdims=True), (1, GATE_COLS)))
    rows.append(jnp.zeros((8 - N_HEADS, GATE_COLS), F32))
    return jnp.concatenate(rows, axis=0)


def _in_proj_kernel(x_ref, g_ref, b_ref, wf_ref, ws_ref, wg_ref, bf_ref, tril_ref,
                    qf_ref, kf_ref, vf_ref, qs_ref, ks_ref, vs_ref, nb_ref, kn_ref,
                    carry_ref, *, tiles_per_seq):
    i = pl.program_id(0)
    scale = 1.0 / math.sqrt(HEAD_DIM)
    hn = _layer_norm(x_ref[...], g_ref[...], b_ref[...])
    hb = hn.astype(BF16)
    outs = ((qf_ref, kf_ref, vf_ref), (qs_ref, ks_ref, vs_ref))
    for w_ref, group in zip((wf_ref, ws_ref), outs):
        for n, o_ref in enumerate(group):
            y = jnp.dot(hb, w_ref[:, n * D_GROUP:(n + 1) * D_GROUP],
                        preferred_element_type=F32)
            if n == 0:
                y = y * scale
            yb = y.astype(BF16)
            o_ref[...] = yb
            if o_ref is kf_ref:
                kn_ref[0] = _max_head_norms(yb)

    f_logit = jnp.dot(hb, wg_ref[...], preferred_element_type=F32)
    log_f = _log_sigmoid(f_logit + bf_ref[...])

    @pl.when(i % tiles_per_seq == 0)
    def _():
        carry_ref[...] = jnp.zeros_like(carry_ref)

    tril = tril_ref[...]
    hi, mid, lo = _split3(log_f)
    c = (jnp.dot(tril, hi, preferred_element_type=F32)
         + jnp.dot(tril, mid, preferred_element_type=F32)
         + jnp.dot(tril, lo, preferred_element_type=F32))
    c = c + carry_ref[0:1, :]
    carry_ref[...] = jnp.broadcast_to(c[-1:, :], carry_ref.shape)
    nb_ref[...] = -jnp.transpose(c)[0:8, :]


def _in_proj(x2d, ln_g, ln_b, w_fox, w_sb, w_gate, bf_row, *, rows_per_seq, tm):
    rows, d = x2d.shape
    n_tiles = rows // tm
    tril = jnp.tril(jnp.ones((tm, tm), BF16))
    qkv_shape = jax.ShapeDtypeStruct((rows, D_GROUP), BF16)
    qkv_spec = pl.BlockSpec((tm, D_GROUP), lambda i: (i, 0))
    return pl.pallas_call(
        functools.partial(_in_proj_kernel, tiles_per_seq=rows_per_seq // tm),
        grid=(n_tiles,),
        in_specs=[
            pl.BlockSpec((tm, d), lambda i: (i, 0)),
            _resident((1, d)), _resident((1, d)),
            _resident(w_fox.shape), _resident(w_sb.shape), _resident(w_gate.shape),
            _resident((1, GATE_COLS)),
            _resident((tm, tm)),
        ],
        out_specs=[qkv_spec] * 6 + [pl.BlockSpec((8, tm), lambda i: (0, i)),
                                    pl.BlockSpec((1, 8, GATE_COLS), lambda i: (i, 0, 0))],
        out_shape=[qkv_shape] * 6 + [jax.ShapeDtypeStruct((8, rows), F32),
                                     jax.ShapeDtypeStruct((n_tiles, 8, GATE_COLS), F32)],
        scratch_shapes=[pltpu.VMEM((8, GATE_COLS), F32)],
        compiler_params=pltpu.CompilerParams(
            dimension_semantics=("arbitrary",), vmem_limit_bytes=VMEM_LIMIT),
        name="in_proj",
    )(x2d, ln_g, ln_b, w_fox, w_sb, w_gate, bf_row, tril)


def _tile_ids(tq, tk):
    row = lax.broadcasted_iota(jnp.int32, (tq, tk), 0)
    col = lax.broadcasted_iota(jnp.int32, (tq, tk), 1)
    return row, col


def _row_group(tq):
    return 256 if tq % 256 == 0 else tq


def _qk(q, k):
    return lax.dot_general(q, k, (((1,), (1,)), ((), ())), preferred_element_type=F32)


def _fox_kernel(*refs, tq, tk, has_prefix):
    if has_prefix:
        (q_ref, k_ref, v_ref, nb_ref, kn_ref, nbe_ref, kp_ref, vp_ref, pb_ref, pkn_ref,
         o_ref, m_ref, l_ref, acc_ref) = refs
    else:
        q_ref, k_ref, v_ref, nb_ref, kn_ref, nbe_ref, o_ref, m_ref, l_ref, acc_ref = refs
    h = pl.program_id(1)
    i = pl.program_id(2)
    blocks_per_tile = tq // tk
    base = (pl.program_id(0) * N_HEADS + h) * (pl.num_programs(2) * blocks_per_tile)
    hs = _row_group(tq)
    n_split = tq // hs
    groups = [slice(r * hs, (r + 1) * hs) for r in range(n_split)]
    q_parts = [q_ref[0, g, :] for g in groups]

    def max_norm(qp):
        qf = qp.astype(F32)
        return jnp.max(jnp.sqrt(jnp.sum(qf * qf, axis=-1, keepdims=True)))

    q_norm = functools.reduce(jnp.maximum, [max_norm(qp) for qp in q_parts])

    def logit_gap(key_norm, bias):
        return q_norm * key_norm + bias - jnp.min(m_ref[...])

    def block_gap(j):
        jc = jnp.maximum(j, 0)
        return logit_gap(kn_ref[base + jc], nbe_ref[base + jc])

    def reachable(j, gap):
        return jnp.logical_and(j >= 0, gap >= -EXIT_THRESHOLD).astype(jnp.int32)

    def first_update(g, a, v):
        m = jnp.max(a, axis=-1, keepdims=True)
        p = jnp.exp(a - m)
        m_ref[g, :] = m
        l_ref[g, :] = jnp.sum(p, axis=-1, keepdims=True)
        acc_ref[g, :] = jnp.dot(p.astype(BF16), v, preferred_element_type=F32)

    def update(g, a, v):
        m = m_ref[g, :]
        m_new = jnp.maximum(m, jnp.max(a, axis=-1, keepdims=True))
        alpha = jnp.exp(m - m_new)
        p = jnp.exp(a - m_new)
        m_ref[g, :] = m_new
        l_ref[g, :] = alpha * l_ref[g, :] + jnp.sum(p, axis=-1, keepdims=True)
        acc_ref[g, :] = alpha * acc_ref[g, :] + jnp.dot(p.astype(BF16), v,
                                                        preferred_element_type=F32)

    def update_fixed_max(g, a, v):
        p = jnp.exp(a - m_ref[g, :])
        l_ref[g, :] += jnp.sum(p, axis=-1, keepdims=True)
        acc_ref[g, :] += jnp.dot(p.astype(BF16), v, preferred_element_type=F32)

    def block(j, step):
        start = pl.multiple_of(j * tk, tk)
        k = k_ref[0, pl.ds(start, tk), :]
        v = v_ref[0, pl.ds(start, tk), :]
        bias = nb_ref[0, 0, pl.ds(j, 1), :]
        for g, qp in zip(groups, q_parts):
            step(g, _qk(qp, k) + bias, v)

    def tile_bias(width):
        parts = [nb_ref[0, 0, pl.ds(i * blocks_per_tile + c, 1), 0:min(tk, width - c * tk)]
                 for c in range(-(-width // tk))]
        return parts[0] if len(parts) == 1 else jnp.concatenate(parts, axis=1)

    start = pl.multiple_of(i * tq, tq)
    for r, (g, qp) in enumerate(zip(groups, q_parts)):
        width = (r + 1) * hs
        row, col = _tile_ids(hs, width)
        a = _qk(qp, k_ref[0, pl.ds(start, width), :]) + tile_bias(width)
        first_update(g, jnp.where(col <= row + r * hs, a, NEG_BIG),
                     v_ref[0, pl.ds(start, width), :])

    def body(state):
        j, _, gap = state
        fixed = gap <= FIXED_MAX_GAP

        @pl.when(fixed)
        def _():
            block(j, update_fixed_max)

        @pl.when(jnp.logical_not(fixed))
        def _():
            block(j, update)

        gap = block_gap(j - 1)
        return j - 1, reachable(j - 1, gap), gap

    last = i * blocks_per_tile - 1
    gap = block_gap(last)
    lax.while_loop(lambda s: s[1] != 0, body, (last, reachable(last, gap), gap))
    if has_prefix:
        @pl.when(logit_gap(pkn_ref[h], 0.0) >= -EXIT_THRESHOLD)
        def _():
            k, v, bias = kp_ref[...], vp_ref[...], pb_ref[pl.ds(h, 1), :]
            for g, qp in zip(groups, q_parts):
                update(g, _qk(qp, k) + bias, v)
    o_ref[0] = acc_ref[...] / l_ref[...]


def _sb_kernel(*refs, tq, tk, single_tile, has_prefix):
    if has_prefix:
        q_ref, k_ref, v_ref, tri_ref, kp_ref, vp_ref, trip_ref, o_ref, run_ref = refs
    else:
        q_ref, k_ref, v_ref, tri_ref, o_ref, run_ref = refs
    i = pl.program_id(2)
    n_split = tq // tk
    first = i * n_split
    groups = [slice(r * tk, (r + 1) * tk) for r in range(n_split)]
    q_parts = [q_ref[0, g, :] for g in groups]

    def update(g, z, v, tri, valid, fresh=False):
        run = jnp.zeros((tk, 1), F32) if fresh else run_ref[g, :]
        chunk = tri.shape[0]
        log_beta = _log_sigmoid(z)
        log_1m = log_beta - z
        if valid is not None:
            log_1m = jnp.where(valid, log_1m, 0.0)
        rests = []
        for c in reversed(range(z.shape[1] // chunk)):
            x = log_1m[:, c * chunk:(c + 1) * chunk]
            hi = x.astype(BF16)
            lo = (x - hi.astype(F32)).astype(BF16)
            rests.append(jnp.dot(hi, tri, preferred_element_type=F32)
                         + jnp.dot(lo, tri, preferred_element_type=F32) + run)
            run = run + jnp.sum(x, axis=-1, keepdims=True)
        rest = rests[0] if len(rests) == 1 else jnp.concatenate(rests[::-1], axis=1)
        a = jnp.exp(log_beta + rest)
        if valid is not None:
            a = jnp.where(valid, a, 0.0)
        pv = jnp.dot(a.astype(BF16), v, preferred_element_type=F32)
        o_ref[0, g, :] = pv if fresh else o_ref[0, g, :] + pv
        run_ref[g, :] = run

    def reachable(g):
        return jnp.max(run_ref[g, :]) >= -EXIT_THRESHOLD

    def own_block(g, qp, j):
        start = pl.multiple_of(j * tk, tk)
        row, col = _tile_ids(tk, tk)
        update(g, _qk(qp, k_ref[0, pl.ds(start, tk), :]), v_ref[0, pl.ds(start, tk), :],
               tri_ref[...], col < row, fresh=True)

    def own_and_previous_block(g, qp, j):
        start = pl.multiple_of((j - 1) * tk, tk)
        row, col = _tile_ids(tk, 2 * tk)
        update(g, _qk(qp, k_ref[0, pl.ds(start, 2 * tk), :]),
               v_ref[0, pl.ds(start, 2 * tk), :], tri_ref[...], col < row + tk, fresh=True)

    def spans(with_previous):
        for r, (g, qp) in enumerate(zip(groups, q_parts)):
            if r > 0 or with_previous:
                own_and_previous_block(g, qp, first + r)
            else:
                own_block(g, qp, first)

    if single_tile:
        spans(False)
    else:
        pl.when(i > 0)(lambda: spans(True))
        pl.when(i == 0)(lambda: spans(False))

    def walk_flags(n):
        return [jnp.logical_and(first + r - 2 - n >= 0, reachable(g))
                for r, g in enumerate(groups)]

    def any_flag(flags):
        return functools.reduce(jnp.logical_or, flags).astype(jnp.int32)

    def body(state):
        n, _ = state
        for r, (g, qp, go) in enumerate(zip(groups, q_parts, walk_flags(n))):
            @pl.when(go)
            def _(g=g, qp=qp, j=first + r - 2 - n):
                start = pl.multiple_of(jnp.maximum(j, 0) * tk, tk)
                update(g, _qk(qp, k_ref[0, pl.ds(start, tk), :]),
                       v_ref[0, pl.ds(start, tk), :], tri_ref[...], None)
        return n + 1, any_flag(walk_flags(n + 1))

    lax.while_loop(lambda s: s[1] != 0, body, (jnp.int32(0), any_flag(walk_flags(0))))
    if has_prefix:
        @pl.when(any_flag([reachable(g) for g in groups]) != 0)
        def _():
            _, col = _tile_ids(tk, META_ROWS)
            for g, qp in zip(groups, q_parts):
                update(g, _qk(qp, kp_ref[...]), vp_ref[...], trip_ref[...], col < N_META)


def _strict_lower(n):
    return jnp.tril(jnp.ones((n, n), BF16), k=-1)


def _attention_specs(batch, seq, t):
    q_spec = pl.BlockSpec((1, t, HEAD_DIM), lambda b, h, i: (b, i, h))
    kv_spec = pl.BlockSpec((1, seq, HEAD_DIM), lambda b, h, i: (b, 0, h))
    o_spec = pl.BlockSpec((1, t, HEAD_DIM), lambda b, h, i: (b, i, h))
    out_shape = jax.ShapeDtypeStruct((batch, seq, D_GROUP), F32)
    params = pltpu.CompilerParams(
        dimension_semantics=("arbitrary", "arbitrary", "arbitrary"),
        vmem_limit_bytes=VMEM_LIMIT)
    return q_spec, kv_spec, o_spec, out_shape, params


def _prefix_spec():
    return pl.BlockSpec((META_ROWS, HEAD_DIM), lambda b, h, i: (0, h))


def _fox_attention(q, k, v, nb, key_norm, prefix, *, tq, tk):
    batch, seq, _ = q.shape
    q_spec, kv_spec, o_spec, out_shape, params = _attention_specs(batch, seq, tq)
    nb4 = nb.reshape(8, batch, seq // tk, tk)
    nb_end = jnp.transpose(nb4[:N_HEADS, :, :, tk - 1], (1, 0, 2))
    smem = pl.BlockSpec(memory_space=pltpu.SMEM)
    in_specs = [q_spec, kv_spec, kv_spec,
                pl.BlockSpec((1, 1, seq // tk, tk), lambda b, h, i: (h, b, 0, 0)), smem, smem]
    args = [q, k, v, nb4, key_norm.reshape(-1), nb_end.reshape(-1)]
    if prefix is not None:
        in_specs += [_prefix_spec(), _prefix_spec(), _resident((8, META_ROWS)), smem]
        args += list(prefix)
    return pl.pallas_call(
        functools.partial(_fox_kernel, tq=tq, tk=tk, has_prefix=prefix is not None),
        grid=(batch, N_HEADS, seq // tq),
        in_specs=in_specs, out_specs=o_spec, out_shape=out_shape,
        scratch_shapes=[pltpu.VMEM((tq, 1), F32), pltpu.VMEM((tq, 1), F32),
                        pltpu.VMEM((tq, HEAD_DIM), F32)],
        compiler_params=params, name="fox_attention",
    )(*args)


def _sb_attention(q, k, v, prefix, *, tq, tk):
    batch, seq, _ = q.shape
    q_spec, kv_spec, o_spec, out_shape, params = _attention_specs(batch, seq, tq)
    in_specs = [q_spec, kv_spec, kv_spec, _resident((tk, tk))]
    args = [q, k, v, _strict_lower(tk)]
    if prefix is not None:
        in_specs += [_prefix_spec(), _prefix_spec(), _resident((META_ROWS, META_ROWS))]
        args += list(prefix) + [_strict_lower(META_ROWS)]
    return pl.pallas_call(
        functools.partial(_sb_kernel, tq=tq, tk=tk, single_tile=seq == tq,
                          has_prefix=prefix is not None),
        grid=(batch, N_HEADS, seq // tq),
        in_specs=in_specs, out_specs=o_spec, out_shape=out_shape,
        scratch_shapes=[pltpu.VMEM((tq, 1), F32)],
        compiler_params=params, name="sb_attention",
    )(*args)


def _out_proj_kernel(of_ref, os_ref, x_ref, g0_ref, b0_ref, gf_ref, gs_ref, wo_ref,
                     g1_ref, b1_ref, h_ref):
    mf = _rms_norm(of_ref[...], gf_ref[...]).astype(BF16)
    ms = _rms_norm(os_ref[...], gs_ref[...]).astype(BF16)
    y = (jnp.dot(mf, wo_ref[0:D_GROUP, :], preferred_element_type=F32)
         + jnp.dot(ms, wo_ref[D_GROUP:, :], preferred_element_type=F32))
    h0 = _layer_norm(x_ref[...], g0_ref[...], b0_ref[...])
    h_ref[...] = _layer_norm(DEEPNORM_ALPHA * h0 + y, g1_ref[...], b1_ref[...])


def _out_proj(o_f, o_s, x2d, ln0_g, ln0_b, g_fox, g_sb, w_o, ln1_g, ln1_b, *, tm):
    rows, d = x2d.shape
    row_spec = lambda width: pl.BlockSpec((tm, width), lambda i: (i, 0))
    return pl.pallas_call(
        _out_proj_kernel,
        grid=(rows // tm,),
        in_specs=[row_spec(D_GROUP), row_spec(D_GROUP), row_spec(d),
                  _resident((1, d)), _resident((1, d)),
                  _resident((1, D_GROUP)), _resident((1, D_GROUP)),
                  _resident(w_o.shape), _resident((1, d)), _resident((1, d))],
        out_specs=row_spec(d),
        out_shape=jax.ShapeDtypeStruct((rows, d), F32),
        compiler_params=pltpu.CompilerParams(
            dimension_semantics=("arbitrary",), vmem_limit_bytes=VMEM_LIMIT),
        name="out_proj",
    )(o_f, o_s, x2d, ln0_g, ln0_b, g_fox, g_sb, w_o, ln1_g, ln1_b)


def _gate_kernel(h_ref, w_ref, g_ref):
    g_ref[...] = jnp.dot(h_ref[...].astype(BF16), w_ref[...], preferred_element_type=F32)


def _gate_rows(h_rows, w_gu, d_ff, *, tf):
    rows, d = h_rows.shape
    return pl.pallas_call(
        _gate_kernel,
        grid=(d_ff // tf,),
        in_specs=[_resident((rows, d)), pl.BlockSpec((d, tf), lambda f: (0, f))],
        out_specs=pl.BlockSpec((rows, tf), lambda f: (0, f)),
        out_shape=jax.ShapeDtypeStruct((rows, d_ff), F32),
        compiler_params=pltpu.CompilerParams(dimension_semantics=("arbitrary",)),
        name="gate_rows",
    )(h_rows, w_gu)


def _ffn_kernel(h_ref, wg_ref, wu_ref, wd_ref, cw_ref, cb_ref, gi_ref, g2_ref, b2_ref,
                o_ref, hb_ref, acc_ref, hist_ref, *, tm):
    i = pl.program_id(1)
    f = pl.program_id(2)

    @pl.when(f == 0)
    def _():
        hb_ref[...] = h_ref[0].astype(BF16)
        acc_ref[...] = jnp.zeros_like(acc_ref)

    @pl.when(i == 0)
    def _():
        hist_ref[f] = gi_ref[...]

    hb = hb_ref[...]
    g = jnp.dot(hb, wg_ref[...], preferred_element_type=F32)
    u = jnp.dot(hb, wu_ref[...], preferred_element_type=F32)
    hist = hist_ref[f]
    prev1 = hist[7:8, :]
    prev2 = hist[6:7, :]
    hist_ref[f] = g[tm - 8:, :]
    row = lax.broadcasted_iota(jnp.int32, g.shape, 0)
    g1 = jnp.where(row == 0, prev1, pltpu.roll(g, 1, 0))
    g2 = jnp.where(row == 0, prev2, jnp.where(row == 1, prev1, pltpu.roll(g, 2, 0)))
    cw = cw_ref[...]
    gc = cw[0:1, :] * g2 + cw[1:2, :] * g1 + cw[2:3, :] * g + cb_ref[...]
    act = gc * (1.0 / (1.0 + jnp.exp(-gc))) * u
    acc_ref[...] += jnp.dot(act.astype(BF16), wd_ref[...], preferred_element_type=F32)

    @pl.when(f == pl.num_programs(2) - 1)
    def _():
        o_ref[0] = _layer_norm(DEEPNORM_ALPHA * h_ref[0] + acc_ref[...],
                               g2_ref[...], b2_ref[...])


def _ffn(h1, w_gu, w_down, conv_w, conv_b, g_init, ln2_g, ln2_b, *, tm, tf):
    batch, seq, d = h1.shape
    d_ff = w_down.shape[0]
    n_f = d_ff // tf
    return pl.pallas_call(
        functools.partial(_ffn_kernel, tm=tm),
        grid=(batch, seq // tm, n_f),
        in_specs=[
            pl.BlockSpec((1, tm, d), lambda b, i, f: (b, i, 0)),
            pl.BlockSpec((d, tf), lambda b, i, f: (0, f)),
            pl.BlockSpec((d, tf), lambda b, i, f: (0, n_f + f)),
            pl.BlockSpec((tf, d), lambda b, i, f: (f, 0)),
            pl.BlockSpec((CONV_WIDTH, tf), lambda b, i, f: (0, f)),
            pl.BlockSpec((1, tf), lambda b, i, f: (0, f)),
            pl.BlockSpec((8, tf), lambda b, i, f: (0, f)),
            _resident((1, d)), _resident((1, d)),
        ],
        out_specs=pl.BlockSpec((1, tm, d), lambda b, i, f: (b, i, 0)),
        out_shape=jax.ShapeDtypeStruct((batch, seq, d), F32),
        scratch_shapes=[pltpu.VMEM((tm, d), BF16), pltpu.VMEM((tm, d), F32),
                        pltpu.VMEM((n_f, 8, tf), F32)],
        compiler_params=pltpu.CompilerParams(
            dimension_semantics=("arbitrary", "arbitrary", "arbitrary"),
            vmem_limit_bytes=VMEM_LIMIT),
        name="conv_ffn",
    )(h1, w_gu, w_gu, w_down, conv_w, conv_b, g_init, ln2_g, ln2_b)


def _pick(n, preferred):
    return preferred if n % preferred == 0 else n


def kernel(x, meta, ln0_g, ln0_b, w_in, b_f, g_fox, g_sb, w_o, ln1_g, ln1_b, w_gu, conv_w,
           conv_b, w_down, ln2_g, ln2_b):
    batch, seq, d = x.shape
    assert w_in.shape[0] == 1, "single layer"
    assert d == 2 * D_GROUP
    rows = batch * seq
    d_ff = w_down.shape[1]
    tm = _pick(seq, 512)
    tq_att = _pick(seq, 1024)
    tk_sb = _pick(seq, 256)
    tm_ffn = _pick(seq, 512)
    tf = _pick(d_ff, 512)

    row = lambda a: a.reshape(1, -1).astype(F32)
    w_fox, w_sb, w_gate = _split_w_in(w_in.astype(F32), tr=256)
    bf_row = jnp.pad(b_f[0].astype(F32), (0, GATE_COLS - N_HEADS)).reshape(1, GATE_COLS)
    w_o_b = w_o[0].astype(BF16)
    w_gu_b = w_gu[0].astype(BF16)
    w_down_b = w_down[0].astype(BF16)
    ln0 = (row(ln0_g), row(ln0_b))
    ln1 = (row(ln1_g[0]), row(ln1_b[0]))
    ln2 = (row(ln2_g[0]), row(ln2_b[0]))
    gf, gs = row(g_fox[0]), row(g_sb[0])

    def mixer(x2d, seqs, seq_len, t, tq, tk, meta_keys):
        qf, kf, vf, qs, ks, vs, nb, kn = _in_proj(x2d, *ln0, w_fox, w_sb, w_gate, bf_row,
                                                  rows_per_seq=seq_len, tm=t)
        kn = jnp.sqrt(kn[:, :N_HEADS, 0]).reshape(seqs, seq_len // t, N_HEADS)
        kn = lax.cummax(jnp.transpose(kn, (0, 2, 1)), axis=2)
        prefix_f = prefix_s = None
        if meta_keys is not None:
            kf_m, vf_m, nb_m, kn_m, ks_m, vs_m = meta_keys
            kn = jnp.maximum(kn, kn_m[0, :, :1])
            valid = jnp.arange(META_ROWS)[None, :] < N_META
            pbias = jnp.where(valid, nb_m - nb_m[:, N_META - 1:N_META], NEG_BIG)
            prefix_f = (kf_m, vf_m, pbias, kn_m.reshape(-1))
            prefix_s = (ks_m, vs_m)
        shape3 = (seqs, seq_len, D_GROUP)
        o_f = _fox_attention(qf.reshape(shape3), kf.reshape(shape3), vf.reshape(shape3),
                             nb, kn, prefix_f, tq=tq, tk=t)
        o_s = _sb_attention(qs.reshape(shape3), ks.reshape(shape3), vs.reshape(shape3),
                            prefix_s, tq=tq, tk=tk)
        h1 = _out_proj(o_f.reshape(-1, D_GROUP), o_s.reshape(-1, D_GROUP), x2d, *ln0,
                       gf, gs, w_o_b, *ln1, tm=t)
        return h1, (kf, vf, nb, kn, ks, vs)

    meta_pad = jnp.zeros((META_ROWS, d), F32).at[:N_META].set(meta.astype(F32))
    h1_meta, meta_keys = mixer(meta_pad, 1, META_ROWS, META_ROWS, META_ROWS, META_ROWS, None)
    g_meta = _gate_rows(h1_meta[:N_META], w_gu_b, d_ff, tf=tf)
    g_init = g_meta[N_META - 8:N_META]

    h1, _ = mixer(x.reshape(rows, d).astype(F32), batch, seq, tm, tq_att, tk_sb, meta_keys)
    out = _ffn(h1.reshape(batch, seq, d), w_gu_b, w_down_b, conv_w[0].astype(F32),
               row(conv_b[0]), g_init, *ln2, tm=tm_ffn, tf=tf)
    return out.astype(x.dtype)
```

```python
import functools
import math

import jax
import jax.numpy as jnp
from jax import lax
from jax.experimental import pallas as pl
from jax.experimental.pallas import tpu as pltpu

N_META = 16
HEAD_DIM = 256
N_HEADS = 4
D_GROUP = N_HEADS * HEAD_DIM
CONV_WIDTH = 3
LN_EPS = 1e-5
RMS_EPS = 1e-6
DEEPNORM_ALPHA = 2.0 ** 0.25
GATE_COLS = 128
META_ROWS = 128
NEG_BIG = -1e30
EXIT_THRESHOLD = 110.0
FIXED_MAX_GAP = 40.0

F32 = jnp.float32
BF16 = jnp.bfloat16

VMEM_LIMIT = 56 * 1024 * 1024


def _layer_norm(x, g, b):
    mu = jnp.mean(x, axis=-1, keepdims=True)
    xc = x - mu
    var = jnp.mean(xc * xc, axis=-1, keepdims=True)
    return xc * lax.rsqrt(var + LN_EPS) * g + b


def _rms_norm(x, g):
    ms = jnp.mean(x * x, axis=-1, keepdims=True)
    return x * lax.rsqrt(ms + RMS_EPS) * g


def _log_sigmoid(z):
    return jnp.minimum(z, 0.0) - jnp.log(1.0 + jnp.exp(-jnp.abs(z)))


def _split3(x):
    hi = x.astype(BF16)
    r1 = x - hi.astype(F32)
    mid = r1.astype(BF16)
    lo = (r1 - mid.astype(F32)).astype(BF16)
    return hi, mid, lo


def _resident(shape):
    nd = len(shape)
    return pl.BlockSpec(shape, lambda *_: (0,) * nd, pipeline_mode=pl.Buffered(1))


def _split_w_in_kernel(f_ref, s0_ref, s1_ref, wf_ref, ws_ref, wg_ref):
    i = pl.program_id(0)
    wf_ref[...] = jnp.transpose(f_ref[...]).astype(BF16)
    s0 = s0_ref[...]
    sb = jnp.concatenate([s0[N_HEADS:, :], s1_ref[0:N_HEADS, :]], axis=0)
    ws_ref[...] = jnp.transpose(sb).astype(BF16)

    @pl.when(i == 0)
    def _():
        gate = s0[0:GATE_COLS, :]
        row = lax.broadcasted_iota(jnp.int32, gate.shape, 0)
        wg_ref[...] = jnp.transpose(jnp.where(row < N_HEADS, gate, 0.0)).astype(BF16)


def _split_w_in(w_t, *, tc):
    cols, d = w_t.shape
    n_c = 3 * D_GROUP // tc
    in_spec = lambda first: pl.BlockSpec((tc, d), lambda i: (first + i, 0))
    out_spec = pl.BlockSpec((d, tc), lambda i: (0, i))
    return pl.pallas_call(
        _split_w_in_kernel,
        grid=(n_c,),
        in_specs=[in_spec(0), in_spec(n_c), in_spec(n_c + 1)],
        out_specs=[out_spec, out_spec, pl.BlockSpec((d, GATE_COLS), lambda i: (0, 0))],
        out_shape=[jax.ShapeDtypeStruct((d, 3 * D_GROUP), BF16),
                   jax.ShapeDtypeStruct((d, 3 * D_GROUP), BF16),
                   jax.ShapeDtypeStruct((d, GATE_COLS), BF16)],
        compiler_params=pltpu.CompilerParams(
            dimension_semantics=("arbitrary",), vmem_limit_bytes=VMEM_LIMIT),
        name="split_w_in",
    )(w_t, w_t, w_t)


def _max_head_norms(yb):
    y = yb.astype(F32)
    sq = y * y
    rows = []
    for h in range(N_HEADS):
        n2 = jnp.sum(sq[:, h * HEAD_DIM:(h + 1) * HEAD_DIM], axis=-1, keepdims=True)
        rows.append(jnp.broadcast_to(jnp.max(n2, axis=0, keepdims=True), (1, GATE_COLS)))
    rows.append(jnp.zeros((8 - N_HEADS, GATE_COLS), F32))
    return jnp.concatenate(rows, axis=0)


def _in_proj_kernel(x_ref, g_ref, b_ref, wf_ref, ws_ref, wg_ref, bf_ref, tril_ref,
                    qf_ref, kf_ref, vf_ref, qs_ref, ks_ref, vs_ref, nb_ref, kn_ref,
                    carry_ref, *, tiles_per_seq):
    i = pl.program_id(0)
    scale = 1.0 / math.sqrt(HEAD_DIM)
    hn = _layer_norm(x_ref[...], g_ref[...], b_ref[...])
    hb = hn.astype(BF16)
    outs = ((qf_ref, kf_ref, vf_ref), (qs_ref, ks_ref, vs_ref))
    for w_ref, group in zip((wf_ref, ws_ref), outs):
        for n, o_ref in enumerate(group):
            y = jnp.dot(hb, w_ref[:, n * D_GROUP:(n + 1) * D_GROUP],
                        preferred_element_type=F32)
            if n == 0:
                y = y * scale
            yb = y.astype(BF16)
            o_ref[...] = yb
            if o_ref is kf_ref:
                kn_ref[0] = _max_head_norms(yb)

    f_logit = jnp.dot(hb, wg_ref[...], preferred_element_type=F32)
    log_f = _log_sigmoid(f_logit + bf_ref[...])

    @pl.when(i % tiles_per_seq == 0)
    def _():
        carry_ref[...] = jnp.zeros_like(carry_ref)

    tril = tril_ref[...]
    hi, mid, lo = _split3(log_f)
    c = (jnp.dot(tril, hi, preferred_element_type=F32)
         + jnp.dot(tril, mid, preferred_element_type=F32)
         + jnp.dot(tril, lo, preferred_element_type=F32))
    c = c + carry_ref[0:1, :]
    carry_ref[...] = jnp.broadcast_to(c[-1:, :], carry_ref.shape)
    nb_ref[...] = -jnp.transpose(c)[0:8, :]


def _in_proj(x2d, ln_g, ln_b, w_fox, w_sb, w_gate, bf_row, *, rows_per_seq, tm):
    rows, d = x2d.shape
    n_tiles = rows // tm
    tril = jnp.tril(jnp.ones((tm, tm), BF16))
    qkv_shape = jax.ShapeDtypeStruct((rows, D_GROUP), BF16)
    qkv_spec = pl.BlockSpec((tm, D_GROUP), lambda i: (i, 0))
    return pl.pallas_call(
        functools.partial(_in_proj_kernel, tiles_per_seq=rows_per_seq // tm),
        grid=(n_tiles,),
        in_specs=[
            pl.BlockSpec((tm, d), lambda i: (i, 0)),
            _resident((1, d)), _resident((1, d)),
            _resident(w_fox.shape), _resident(w_sb.shape), _resident(w_gate.shape),
            _resident((1, GATE_COLS)),
            _resident((tm, tm)),
        ],
        out_specs=[qkv_spec] * 6 + [pl.BlockSpec((8, tm), lambda i: (0, i)),
                                    pl.BlockSpec((1, 8, GATE_COLS), lambda i: (i, 0, 0))],
        out_shape=[qkv_shape] * 6 + [jax.ShapeDtypeStruct((8, rows), F32),
                                     jax.ShapeDtypeStruct((n_tiles, 8, GATE_COLS), F32)],
        scratch_shapes=[pltpu.VMEM((8, GATE_COLS), F32)],
        compiler_params=pltpu.CompilerParams(
            dimension_semantics=("arbitrary",), vmem_limit_bytes=VMEM_LIMIT),
        name="in_proj",
    )(x2d, ln_g, ln_b, w_fox, w_sb, w_gate, bf_row, tril)


def _tile_ids(tq, tk):
    row = lax.broadcasted_iota(jnp.int32, (tq, tk), 0)
    col = lax.broadcasted_iota(jnp.int32, (tq, tk), 1)
    return row, col


def _row_group(tq):
    return 256 if tq % 256 == 0 else tq


def _qk(q, k):
    return lax.dot_general(q, k, (((1,), (1,)), ((), ())), preferred_element_type=F32)


def _fox_kernel(*refs, tq, tk, has_prefix):
    if has_prefix:
        (q_ref, k_ref, v_ref, nb_ref, kn_ref, nbe_ref, kp_ref, vp_ref, pb_ref, pkn_ref,
         o_ref, m_ref, l_ref, acc_ref) = refs
    else:
        q_ref, k_ref, v_ref, nb_ref, kn_ref, nbe_ref, o_ref, m_ref, l_ref, acc_ref = refs
    h = pl.program_id(1)
    i = pl.program_id(2)
    blocks_per_tile = tq // tk
    base = (pl.program_id(0) * N_HEADS + h) * (pl.num_programs(2) * blocks_per_tile)
    hs = _row_group(tq)
    n_split = tq // hs
    groups = [slice(r * hs, (r + 1) * hs) for r in range(n_split)]
    q_parts = [q_ref[0, g, :] for g in groups]

    def max_norm(qp):
        qf = qp.astype(F32)
        return jnp.max(jnp.sqrt(jnp.sum(qf * qf, axis=-1, keepdims=True)))

    q_norm = functools.reduce(jnp.maximum, [max_norm(qp) for qp in q_parts])

    def logit_gap(key_norm, bias):
        return q_norm * key_norm + bias - jnp.min(m_ref[...])

    def block_gap(j):
        jc = jnp.maximum(j, 0)
        return logit_gap(kn_ref[base + jc], nbe_ref[base + jc])

    def reachable(j, gap):
        return jnp.logical_and(j >= 0, gap >= -EXIT_THRESHOLD).astype(jnp.int32)

    def first_update(g, a, v):
        m = jnp.max(a, axis=-1, keepdims=True)
        p = jnp.exp(a - m)
        m_ref[g, :] = m
        l_ref[g, :] = jnp.sum(p, axis=-1, keepdims=True)
        acc_ref[g, :] = jnp.dot(p.astype(BF16), v, preferred_element_type=F32)

    def update(g, a, v):
        m = m_ref[g, :]
        m_new = jnp.maximum(m, jnp.max(a, axis=-1, keepdims=True))
        alpha = jnp.exp(m - m_new)
        p = jnp.exp(a - m_new)
        m_ref[g, :] = m_new
        l_ref[g, :] = alpha * l_ref[g, :] + jnp.sum(p, axis=-1, keepdims=True)
        acc_ref[g, :] = alpha * acc_ref[g, :] + jnp.dot(p.astype(BF16), v,
                                                        preferred_element_type=F32)

    def update_fixed_max(g, a, v):
        p = jnp.exp(a - m_ref[g, :])
        l_ref[g, :] += jnp.sum(p, axis=-1, keepdims=True)
        acc_ref[g, :] += jnp.dot(p.astype(BF16), v, preferred_element_type=F32)

    def block(j, step):
        start = pl.multiple_of(j * tk, tk)
        k = k_ref[0, pl.ds(start, tk), :]
        v = v_ref[0, pl.ds(start, tk), :]
        bias = nb_ref[0, 0, pl.ds(j, 1), :]
        for g, qp in zip(groups, q_parts):
            step(g, _qk(qp, k) + bias, v)

    def tile_bias(width):
        parts = [nb_ref[0, 0, pl.ds(i * blocks_per_tile + c, 1), 0:min(tk, width - c * tk)]
                 for c in range(-(-width // tk))]
        return parts[0] if len(parts) == 1 else jnp.concatenate(parts, axis=1)

    start = pl.multiple_of(i * tq, tq)
    for r, (g, qp) in enumerate(zip(groups, q_parts)):
        width = (r + 1) * hs
        row, col = _tile_ids(hs, width)
        a = _qk(qp, k_ref[0, pl.ds(start, width), :]) + tile_bias(width)
        first_update(g, jnp.where(col <= row + r * hs, a, NEG_BIG),
                     v_ref[0, pl.ds(start, width), :])

    def body(state):
        j, _, gap = state
        fixed = gap <= FIXED_MAX_GAP

        @pl.when(fixed)
        def _():
            block(j, update_fixed_max)

        @pl.when(jnp.logical_not(fixed))
        def _():
            block(j, update)

        gap = block_gap(j - 1)
        return j - 1, reachable(j - 1, gap), gap

    last = i * blocks_per_tile - 1
    gap = block_gap(last)
    lax.while_loop(lambda s: s[1] != 0, body, (last, reachable(last, gap), gap))
    if has_prefix:
        @pl.when(logit_gap(pkn_ref[h], 0.0) >= -EXIT_THRESHOLD)
        def _():
            k, v, bias = kp_ref[...], vp_ref[...], pb_ref[pl.ds(h, 1), :]
            for g, qp in zip(groups, q_parts):
                update(g, _qk(qp, k) + bias, v)
    o_ref[0] = acc_ref[...] / l_ref[...]


def _sb_kernel(*refs, tq, tk, single_tile, has_prefix):
    if has_prefix:
        q_ref, k_ref, v_ref, tri_ref, kp_ref, vp_ref, trip_ref, o_ref, run_ref = refs
    else:
        q_ref, k_ref, v_ref, tri_ref, o_ref, run_ref = refs
    i = pl.program_id(2)
    n_split = tq // tk
    first = i * n_split
    groups = [slice(r * tk, (r + 1) * tk) for r in range(n_split)]
    q_parts = [q_ref[0, g, :] for g in groups]

    def update(g, z, v, tri, valid, fresh=False):
        run = jnp.zeros((tk, 1), F32) if fresh else run_ref[g, :]
        chunk = tri.shape[0]
        log_beta = _log_sigmoid(z)
        log_1m = log_beta - z
        if valid is not None:
            log_1m = jnp.where(valid, log_1m, 0.0)
        rests = []
        for c in reversed(range(z.shape[1] // chunk)):
            x = log_1m[:, c * chunk:(c + 1) * chunk]
            hi = x.astype(BF16)
            lo = (x - hi.astype(F32)).astype(BF16)
            rests.append(jnp.dot(hi, tri, preferred_element_type=F32)
                         + jnp.dot(lo, tri, preferred_element_type=F32) + run)
            run = run + jnp.sum(x, axis=-1, keepdims=True)
        rest = rests[0] if len(rests) == 1 else jnp.concatenate(rests[::-1], axis=1)
        a = jnp.exp(log_beta + rest)
        if valid is not None:
            a = jnp.where(valid, a, 0.0)
        pv = jnp.dot(a.astype(BF16), v, preferred_element_type=F32)
        o_ref[0, g, :] = pv if fresh else o_ref[0, g, :] + pv
        run_ref[g, :] = run

    def reachable(g):
        return jnp.max(run_ref[g, :]) >= -EXIT_THRESHOLD

    def own_block(g, qp, j):
        start = pl.multiple_of(j * tk, tk)
        row, col = _tile_ids(tk, tk)
        update(g, _qk(qp, k_ref[0, pl.ds(start, tk), :]), v_ref[0, pl.ds(start, tk), :],
               tri_ref[...], col < row, fresh=True)

    def own_and_previous_block(g, qp, j):
        start = pl.multiple_of((j - 1) * tk, tk)
        row, col = _tile_ids(tk, 2 * tk)
        update(g, _qk(qp, k_ref[0, pl.ds(start, 2 * tk), :]),
               v_ref[0, pl.ds(start, 2 * tk), :], tri_ref[...], col < row + tk, fresh=True)

    def spans(with_previous):
        for r, (g, qp) in enumerate(zip(groups, q_parts)):
            if r > 0 or with_previous:
                own_and_previous_block(g, qp, first + r)
            else:
                own_block(g, qp, first)

    if single_tile:
        spans(False)
    else:
        pl.when(i > 0)(lambda: spans(True))
        pl.when(i == 0)(lambda: spans(False))

    def walk_flags(n):
        return [jnp.logical_and(first + r - 2 - n >= 0, reachable(g))
                for r, g in enumerate(groups)]

    def any_flag(flags):
        return functools.reduce(jnp.logical_or, flags).astype(jnp.int32)

    def body(state):
        n, _ = state
        for r, (g, qp, go) in enumerate(zip(groups, q_parts, walk_flags(n))):
            @pl.when(go)
            def _(g=g, qp=qp, j=first + r - 2 - n):
                start = pl.multiple_of(jnp.maximum(j, 0) * tk, tk)
                update(g, _qk(qp, k_ref[0, pl.ds(start, tk), :]),
                       v_ref[0, pl.ds(start, tk), :], tri_ref[...], None)
        return n + 1, any_flag(walk_flags(n + 1))

    lax.while_loop(lambda s: s[1] != 0, body, (jnp.int32(0), any_flag(walk_flags(0))))
    if has_prefix:
        @pl.when(any_flag([reachable(g) for g in groups]) != 0)
        def _():
            _, col = _tile_ids(tk, META_ROWS)
            for g, qp in zip(groups, q_parts):
                update(g, _qk(qp, kp_ref[...]), vp_ref[...], trip_ref[...], col < N_META)


def _strict_lower(n):
    return jnp.tril(jnp.ones((n, n), BF16), k=-1)


def _attention_specs(batch, seq, t):
    q_spec = pl.BlockSpec((1, t, HEAD_DIM), lambda b, h, i: (b, i, h))
    kv_spec = pl.BlockSpec((1, seq, HEAD_DIM), lambda b, h, i: (b, 0, h))
    o_spec = pl.BlockSpec((1, t, HEAD_DIM), lambda b, h, i: (b, i, h))
    out_shape = jax.ShapeDtypeStruct((batch, seq, D_GROUP), F32)
    params = pltpu.CompilerParams(
        dimension_semantics=("arbitrary", "arbitrary", "arbitrary"),
        vmem_limit_bytes=VMEM_LIMIT)
    return q_spec, kv_spec, o_spec, out_shape, params


def _prefix_spec():
    return pl.BlockSpec((META_ROWS, HEAD_DIM), lambda b, h, i: (0, h))


def _fox_attention(q, k, v, nb, key_norm, prefix, *, tq, tk):
    batch, seq, _ = q.shape
    q_spec, kv_spec, o_spec, out_shape, params = _attention_specs(batch, seq, tq)
    nb4 = nb.reshape(8, batch, seq // tk, tk)
    nb_end = jnp.transpose(nb4[:N_HEADS, :, :, tk - 1], (1, 0, 2))
    smem = pl.BlockSpec(memory_space=pltpu.SMEM)
    in_specs = [q_spec, kv_spec, kv_spec,
                pl.BlockSpec((1, 1, seq // tk, tk), lambda b, h, i: (h, b, 0, 0)), smem, smem]
    args = [q, k, v, nb4, key_norm.reshape(-1), nb_end.reshape(-1)]
    if prefix is not None:
        in_specs += [_prefix_spec(), _prefix_spec(), _resident((8, META_ROWS)), smem]
        args += list(prefix)
    return pl.pallas_call(
        functools.partial(_fox_kernel, tq=tq, tk=tk, has_prefix=prefix is not None),
        grid=(batch, N_HEADS, seq // tq),
        in_specs=in_specs, out_specs=o_spec, out_shape=out_shape,
        scratch_shapes=[pltpu.VMEM((tq, 1), F32), pltpu.VMEM((tq, 1), F32),
                        pltpu.VMEM((tq, HEAD_DIM), F32)],
        compiler_params=params, name="fox_attention",
    )(*args)


def _sb_attention(q, k, v, prefix, *, tq, tk):
    batch, seq, _ = q.shape
    q_spec, kv_spec, o_spec, out_shape, params = _attention_specs(batch, seq, tq)
    in_specs = [q_spec, kv_spec, kv_spec, _resident((tk, tk))]
    args = [q, k, v, _strict_lower(tk)]
    if prefix is not None:
        in_specs += [_prefix_spec(), _prefix_spec(), _resident((META_ROWS, META_ROWS))]
        args += list(prefix) + [_strict_lower(META_ROWS)]
    return pl.pallas_call(
        functools.partial(_sb_kernel, tq=tq, tk=tk, single_tile=seq == tq,
                          has_prefix=prefix is not None),
        grid=(batch, N_HEADS, seq // tq),
        in_specs=in_specs, out_specs=o_spec, out_shape=out_shape,
        scratch_shapes=[pltpu.VMEM((tq, 1), F32)],
        compiler_params=params, name="sb_attention",
    )(*args)


def _out_proj_kernel(of_ref, os_ref, x_ref, g0_ref, b0_ref, gf_ref, gs_ref, wo_ref,
                     g1_ref, b1_ref, h_ref):
    mf = _rms_norm(of_ref[...], gf_ref[...]).astype(BF16)
    ms = _rms_norm(os_ref[...], gs_ref[...]).astype(BF16)
    y = (jnp.dot(mf, wo_ref[0:D_GROUP, :], preferred_element_type=F32)
         + jnp.dot(ms, wo_ref[D_GROUP:, :], preferred_element_type=F32))
    h0 = _layer_norm(x_ref[...], g0_ref[...], b0_ref[...])
    h_ref[...] = _layer_norm(DEEPNORM_ALPHA * h0 + y, g1_ref[...], b1_ref[...])


def _out_proj(o_f, o_s, x2d, ln0_g, ln0_b, g_fox, g_sb, w_o, ln1_g, ln1_b, *, tm):
    rows, d = x2d.shape
    row_spec = lambda width: pl.BlockSpec((tm, width), lambda i: (i, 0))
    return pl.pallas_call(
        _out_proj_kernel,
        grid=(rows // tm,),
        in_specs=[row_spec(D_GROUP), row_spec(D_GROUP), row_spec(d),
                  _resident((1, d)), _resident((1, d)),
                  _resident((1, D_GROUP)), _resident((1, D_GROUP)),
                  _resident(w_o.shape), _resident((1, d)), _resident((1, d))],
        out_specs=row_spec(d),
        out_shape=jax.ShapeDtypeStruct((rows, d), F32),
        compiler_params=pltpu.CompilerParams(
            dimension_semantics=("arbitrary",), vmem_limit_bytes=VMEM_LIMIT),
        name="out_proj",
    )(o_f, o_s, x2d, ln0_g, ln0_b, g_fox, g_sb, w_o, ln1_g, ln1_b)


def _gate_kernel(h_ref, w_ref, g_ref):
    g_ref[...] = jnp.dot(h_ref[...].astype(BF16), w_ref[...], preferred_element_type=F32)


def _gate_rows(h_rows, w_gu, d_ff, *, tf):
    rows, d = h_rows.shape
    return pl.pallas_call(
        _gate_kernel,
        grid=(d_ff // tf,),
        in_specs=[_resident((rows, d)), pl.BlockSpec((d, tf), lambda f: (0, f))],
        out_specs=pl.BlockSpec((rows, tf), lambda f: (0, f)),
        out_shape=jax.ShapeDtypeStruct((rows, d_ff), F32),
        compiler_params=pltpu.CompilerParams(dimension_semantics=("arbitrary",)),
        name="gate_rows",
    )(h_rows, w_gu)


def _ffn_kernel(h_ref, wg_ref, wu_ref, wd_ref, cw_ref, cb_ref, gi_ref, g2_ref, b2_ref,
                o_ref, hb_ref, acc_ref, hist_ref, *, tm):
    i = pl.program_id(1)
    f = pl.program_id(2)

    @pl.when(f == 0)
    def _():
        hb_ref[...] = h_ref[0].astype(BF16)
        acc_ref[...] = jnp.zeros_like(acc_ref)

    @pl.when(i == 0)
    def _():
        hist_ref[f] = gi_ref[...]

    hb = hb_ref[...]
    g = jnp.dot(hb, wg_ref[...], preferred_element_type=F32)
    u = jnp.dot(hb, wu_ref[...], preferred_element_type=F32)
    hist = hist_ref[f]
    prev1 = hist[7:8, :]
    prev2 = hist[6:7, :]
    hist_ref[f] = g[tm - 8:, :]
    row = lax.broadcasted_iota(jnp.int32, g.shape, 0)
    g1 = jnp.where(row == 0, prev1, pltpu.roll(g, 1, 0))
    g2 = jnp.where(row == 0, prev2, jnp.where(row == 1, prev1, pltpu.roll(g, 2, 0)))
    cw = cw_ref[...]
    gc = cw[0:1, :] * g2 + cw[1:2, :] * g1 + cw[2:3, :] * g + cb_ref[...]
    act = gc * (1.0 / (1.0 + jnp.exp(-gc))) * u
    acc_ref[...] += jnp.dot(act.astype(BF16), wd_ref[...], preferred_element_type=F32)

    @pl.when(f == pl.num_programs(2) - 1)
    def _():
        o_ref[0] = _layer_norm(DEEPNORM_ALPHA * h_ref[0] + acc_ref[...],
                               g2_ref[...], b2_ref[...])


def _ffn(h1, w_gu, w_down, conv_w, conv_b, g_init, ln2_g, ln2_b, *, tm, tf):
    batch, seq, d = h1.shape
    d_ff = w_down.shape[0]
    n_f = d_ff // tf
    return pl.pallas_call(
        functools.partial(_ffn_kernel, tm=tm),
        grid=(batch, seq // tm, n_f),
        in_specs=[
            pl.BlockSpec((1, tm, d), lambda b, i, f: (b, i, 0)),
            pl.BlockSpec((d, tf), lambda b, i, f: (0, f)),
            pl.BlockSpec((d, tf), lambda b, i, f: (0, n_f + f)),
            pl.BlockSpec((tf, d), lambda b, i, f: (f, 0)),
            pl.BlockSpec((CONV_WIDTH, tf), lambda b, i, f: (0, f)),
            pl.BlockSpec((1, tf), lambda b, i, f: (0, f)),
            pl.BlockSpec((8, tf), lambda b, i, f: (0, f)),
            _resident((1, d)), _resident((1, d)),
        ],
        out_specs=pl.BlockSpec((1, tm, d), lambda b, i, f: (b, i, 0)),
        out_shape=jax.ShapeDtypeStruct((batch, seq, d), F32),
        scratch_shapes=[pltpu.VMEM((tm, d), BF16), pltpu.VMEM((tm, d), F32),
                        pltpu.VMEM((n_f, 8, tf), F32)],
        compiler_params=pltpu.CompilerParams(
            dimension_semantics=("arbitrary", "arbitrary", "arbitrary"),
            vmem_limit_bytes=VMEM_LIMIT),
        name="conv_ffn",
    )(h1, w_gu, w_gu, w_down, conv_w, conv_b, g_init, ln2_g, ln2_b)


def _pick(n, preferred):
    return preferred if n % preferred == 0 else n


def kernel(x, meta, ln0_g, ln0_b, w_in, b_f, g_fox, g_sb, w_o, ln1_g, ln1_b, w_gu, conv_w,
           conv_b, w_down, ln2_g, ln2_b):
    batch, seq, d = x.shape
    assert w_in.shape[0] == 1, "single layer"
    assert d == 2 * D_GROUP
    rows = batch * seq
    d_ff = w_down.shape[1]
    tm = _pick(seq, 512)
    tq_fox = _pick(seq, 1024)
    tq_sb = _pick(seq, 2048)
    tk_sb = _pick(seq, 256)
    tm_ffn = _pick(seq, 512)
    tf = _pick(d_ff, 512)

    row = lambda a: a.reshape(1, -1).astype(F32)
    w_fox, w_sb, w_gate = _split_w_in(jnp.transpose(w_in[0].astype(F32)), tc=256)
    bf_row = jnp.pad(b_f[0].astype(F32), (0, GATE_COLS - N_HEADS)).reshape(1, GATE_COLS)
    w_o_b = w_o[0].astype(BF16)
    w_gu_b = w_gu[0].astype(BF16)
    w_down_b = w_down[0].astype(BF16)
    ln0 = (row(ln0_g), row(ln0_b))
    ln1 = (row(ln1_g[0]), row(ln1_b[0]))
    ln2 = (row(ln2_g[0]), row(ln2_b[0]))
    gf, gs = row(g_fox[0]), row(g_sb[0])

    def mixer(x2d, seqs, seq_len, t, tq, tq_sb, tk, meta_keys):
        qf, kf, vf, qs, ks, vs, nb, kn = _in_proj(x2d, *ln0, w_fox, w_sb, w_gate, bf_row,
                                                  rows_per_seq=seq_len, tm=t)
        kn = jnp.sqrt(kn[:, :N_HEADS, 0]).reshape(seqs, seq_len // t, N_HEADS)
        kn = lax.cummax(jnp.transpose(kn, (0, 2, 1)), axis=2)
        prefix_f = prefix_s = None
        if meta_keys is not None:
            kf_m, vf_m, nb_m, kn_m, ks_m, vs_m = meta_keys
            kn = jnp.maximum(kn, kn_m[0, :, :1])
            valid = jnp.arange(META_ROWS)[None, :] < N_META
            pbias = jnp.where(valid, nb_m - nb_m[:, N_META - 1:N_META], NEG_BIG)
            prefix_f = (kf_m, vf_m, pbias, kn_m.reshape(-1))
            prefix_s = (ks_m, vs_m)
        shape3 = (seqs, seq_len, D_GROUP)
        o_f = _fox_attention(qf.reshape(shape3), kf.reshape(shape3), vf.reshape(shape3),
                             nb, kn, prefix_f, tq=tq, tk=t)
        o_s = _sb_attention(qs.reshape(shape3), ks.reshape(shape3), vs.reshape(shape3),
                            prefix_s, tq=tq_sb, tk=tk)
        h1 = _out_proj(o_f.reshape(-1, D_GROUP), o_s.reshape(-1, D_GROUP), x2d, *ln0,
                       gf, gs, w_o_b, *ln1, tm=t)
        return h1, (kf, vf, nb, kn, ks, vs)

    meta_pad = jnp.zeros((META_ROWS, d), F32).at[:N_META].set(meta.astype(F32))
    h1_meta, meta_keys = mixer(meta_pad, 1, META_ROWS, META_ROWS, META_ROWS, META_ROWS,
                               META_ROWS, None)
    g_meta = _gate_rows(h1_meta[:N_META], w_gu_b, d_ff, tf=tf)
    g_init = g_meta[N_META - 8:N_META]

    h1, _ = mixer(x.reshape(rows, d).astype(F32), batch, seq, tm, tq_fox, tq_sb, tk_sb,
                  meta_keys)
    out = _ffn(h1.reshape(batch, seq, d), w_gu_b, w_down_b, conv_w[0].astype(F32),
               row(conv_b[0]), g_init, *ln2, tm=tm_ffn, tf=tf)
    return out.astype(x.dtype)
```

```python
import functools
import math

import jax
import jax.numpy as jnp
from jax import lax
from jax.experimental import pallas as pl
from jax.experimental.pallas import tpu as pltpu

N_META = 16
HEAD_DIM = 256
N_HEADS = 4
D_GROUP = N_HEADS * HEAD_DIM
CONV_WIDTH = 3
LN_EPS = 1e-5
RMS_EPS = 1e-6
DEEPNORM_ALPHA = 2.0 ** 0.25
GATE_COLS = 128
META_ROWS = 128
NEG_BIG = -1e30
EXIT_THRESHOLD = 110.0
FIXED_MAX_GAP = 40.0

F32 = jnp.float32
BF16 = jnp.bfloat16

VMEM_LIMIT = 56 * 1024 * 1024


def _layer_norm(x, g, b):
    mu = jnp.mean(x, axis=-1, keepdims=True)
    xc = x - mu
    var = jnp.mean(xc * xc, axis=-1, keepdims=True)
    return xc * lax.rsqrt(var + LN_EPS) * g + b


def _rms_norm(x, g):
    ms = jnp.mean(x * x, axis=-1, keepdims=True)
    return x * lax.rsqrt(ms + RMS_EPS) * g


def _log_sigmoid(z):
    return jnp.minimum(z, 0.0) - jnp.log(1.0 + jnp.exp(-jnp.abs(z)))


def _split3(x):
    hi = x.astype(BF16)
    r1 = x - hi.astype(F32)
    mid = r1.astype(BF16)
    lo = (r1 - mid.astype(F32)).astype(BF16)
    return hi, mid, lo


def _resident(shape):
    nd = len(shape)
    return pl.BlockSpec(shape, lambda *_: (0,) * nd, pipeline_mode=pl.Buffered(1))


def _split_w_in_kernel(f_ref, s0_ref, s1_ref, wf_ref, ws_ref, wg_ref):
    i = pl.program_id(0)
    wf_ref[...] = jnp.transpose(f_ref[...]).astype(BF16)
    s0 = s0_ref[...]
    sb = jnp.concatenate([s0[N_HEADS:, :], s1_ref[0:N_HEADS, :]], axis=0)
    ws_ref[...] = jnp.transpose(sb).astype(BF16)

    @pl.when(i == 0)
    def _():
        gate = s0[0:GATE_COLS, :]
        row = lax.broadcasted_iota(jnp.int32, gate.shape, 0)
        wg_ref[...] = jnp.transpose(jnp.where(row < N_HEADS, gate, 0.0)).astype(BF16)


def _split_w_in(w_t, *, tc):
    cols, d = w_t.shape
    n_c = 3 * D_GROUP // tc
    in_spec = lambda first: pl.BlockSpec((tc, d), lambda i: (first + i, 0))
    out_spec = pl.BlockSpec((d, tc), lambda i: (0, i))
    return pl.pallas_call(
        _split_w_in_kernel,
        grid=(n_c,),
        in_specs=[in_spec(0), in_spec(n_c), in_spec(n_c + 1)],
        out_specs=[out_spec, out_spec, pl.BlockSpec((d, GATE_COLS), lambda i: (0, 0))],
        out_shape=[jax.ShapeDtypeStruct((d, 3 * D_GROUP), BF16),
                   jax.ShapeDtypeStruct((d, 3 * D_GROUP), BF16),
                   jax.ShapeDtypeStruct((d, GATE_COLS), BF16)],
        compiler_params=pltpu.CompilerParams(
            dimension_semantics=("arbitrary",), vmem_limit_bytes=VMEM_LIMIT),
        name="split_w_in",
    )(w_t, w_t, w_t)


def _max_head_norms(yb):
    y = yb.astype(F32)
    sq = y * y
    rows = []
    for h in range(N_HEADS):
        n2 = jnp.sum(sq[:, h * HEAD_DIM:(h + 1) * HEAD_DIM], axis=-1, keepdims=True)
        rows.append(jnp.broadcast_to(jnp.max(n2, axis=0, keepdims=True), (1, GATE_COLS)))
    rows.append(jnp.zeros((8 - N_HEADS, GATE_COLS), F32))
    return jnp.concatenate(rows, axis=0)


def _in_proj_kernel(x_ref, g_ref, b_ref, wf_ref, ws_ref, wg_ref, bf_ref, tril_ref,
                    qf_ref, kf_ref, vf_ref, qs_ref, ks_ref, vs_ref, nb_ref, kn_ref,
                    carry_ref, *, tiles_per_seq):
    i = pl.program_id(0)
    scale = 1.0 / math.sqrt(HEAD_DIM)
    hn = _layer_norm(x_ref[...], g_ref[...], b_ref[...])
    hb = hn.astype(BF16)
    outs = ((qf_ref, kf_ref, vf_ref), (qs_ref, ks_ref, vs_ref))
    for w_ref, group in zip((wf_ref, ws_ref), outs):
        for n, o_ref in enumerate(group):
            y = jnp.dot(hb, w_ref[:, n * D_GROUP:(n + 1) * D_GROUP],
                        preferred_element_type=F32)
            if n == 0:
                y = y * scale
            yb = y.astype(BF16)
            o_ref[...] = yb
            if o_ref is kf_ref:
                kn_ref[0] = _max_head_norms(yb)

    f_logit = jnp.dot(hb, wg_ref[...], preferred_element_type=F32)
    log_f = _log_sigmoid(f_logit + bf_ref[...])

    @pl.when(i % tiles_per_seq == 0)
    def _():
        carry_ref[...] = jnp.zeros_like(carry_ref)

    tril = tril_ref[...]
    hi, mid, lo = _split3(log_f)
    c = (jnp.dot(tril, hi, preferred_element_type=F32)
         + jnp.dot(tril, mid, preferred_element_type=F32)
         + jnp.dot(tril, lo, preferred_element_type=F32))
    c = c + carry_ref[0:1, :]
    carry_ref[...] = jnp.broadcast_to(c[-1:, :], carry_ref.shape)
    nb_ref[...] = -jnp.transpose(c)[0:8, :]


def _in_proj(x2d, ln_g, ln_b, w_fox, w_sb, w_gate, bf_row, *, rows_per_seq, tm):
    rows, d = x2d.shape
    n_tiles = rows // tm
    tril = jnp.tril(jnp.ones((tm, tm), BF16))
    qkv_shape = jax.ShapeDtypeStruct((rows, D_GROUP), BF16)
    qkv_spec = pl.BlockSpec((tm, D_GROUP), lambda i: (i, 0))
    return pl.pallas_call(
        functools.partial(_in_proj_kernel, tiles_per_seq=rows_per_seq // tm),
        grid=(n_tiles,),
        in_specs=[
            pl.BlockSpec((tm, d), lambda i: (i, 0)),
            _resident((1, d)), _resident((1, d)),
            _resident(w_fox.shape), _resident(w_sb.shape), _resident(w_gate.shape),
            _resident((1, GATE_COLS)),
            _resident((tm, tm)),
        ],
        out_specs=[qkv_spec] * 6 + [pl.BlockSpec((8, tm), lambda i: (0, i)),
                                    pl.BlockSpec((1, 8, GATE_COLS), lambda i: (i, 0, 0))],
        out_shape=[qkv_shape] * 6 + [jax.ShapeDtypeStruct((8, rows), F32),
                                     jax.ShapeDtypeStruct((n_tiles, 8, GATE_COLS), F32)],
        scratch_shapes=[pltpu.VMEM((8, GATE_COLS), F32)],
        compiler_params=pltpu.CompilerParams(
            dimension_semantics=("arbitrary",), vmem_limit_bytes=VMEM_LIMIT),
        name="in_proj",
    )(x2d, ln_g, ln_b, w_fox, w_sb, w_gate, bf_row, tril)


def _tile_ids(tq, tk):
    row = lax.broadcasted_iota(jnp.int32, (tq, tk), 0)
    col = lax.broadcasted_iota(jnp.int32, (tq, tk), 1)
    return row, col


def _row_group(tq):
    return 256 if tq % 256 == 0 else tq


def _qk(q, k):
    return lax.dot_general(q, k, (((1,), (1,)), ((), ())), preferred_element_type=F32)


def _fox_kernel(*refs, tq, tk, has_prefix):
    if has_prefix:
        (q_ref, k_ref, v_ref, nb_ref, kn_ref, nbe_ref, kp_ref, vp_ref, pb_ref, pkn_ref,
         o_ref, m_ref, l_ref, acc_ref) = refs
    else:
        q_ref, k_ref, v_ref, nb_ref, kn_ref, nbe_ref, o_ref, m_ref, l_ref, acc_ref = refs
    h = pl.program_id(1)
    i = pl.program_id(2)
    blocks_per_tile = tq // tk
    base = (pl.program_id(0) * N_HEADS + h) * (pl.num_programs(2) * blocks_per_tile)
    hs = _row_group(tq)
    n_split = tq // hs
    groups = [slice(r * hs, (r + 1) * hs) for r in range(n_split)]
    q_parts = [q_ref[0, g, :] for g in groups]

    def max_norm(qp):
        qf = qp.astype(F32)
        return jnp.max(jnp.sqrt(jnp.sum(qf * qf, axis=-1, keepdims=True)))

    q_norm = functools.reduce(jnp.maximum, [max_norm(qp) for qp in q_parts])

    def logit_gap(key_norm, bias):
        return q_norm * key_norm + bias - jnp.min(m_ref[...])

    def block_gap(j):
        jc = jnp.maximum(j, 0)
        return logit_gap(kn_ref[base + jc], nbe_ref[base + jc])

    def reachable(j, gap):
        return jnp.logical_and(j >= 0, gap >= -EXIT_THRESHOLD).astype(jnp.int32)

    def first_update(g, a, v):
        m = jnp.max(a, axis=-1, keepdims=True)
        p = jnp.exp(a - m)
        m_ref[g, :] = m
        l_ref[g, :] = jnp.sum(p, axis=-1, keepdims=True)
        acc_ref[g, :] = jnp.dot(p.astype(BF16), v, preferred_element_type=F32)

    def update(g, a, v):
        m = m_ref[g, :]
        m_new = jnp.maximum(m, jnp.max(a, axis=-1, keepdims=True))
        alpha = jnp.exp(m - m_new)
        p = jnp.exp(a - m_new)
        m_ref[g, :] = m_new
        l_ref[g, :] = alpha * l_ref[g, :] + jnp.sum(p, axis=-1, keepdims=True)
        acc_ref[g, :] = alpha * acc_ref[g, :] + jnp.dot(p.astype(BF16), v,
                                                        preferred_element_type=F32)

    def update_fixed_max(g, a, v):
        p = jnp.exp(a - m_ref[g, :])
        l_ref[g, :] += jnp.sum(p, axis=-1, keepdims=True)
        acc_ref[g, :] += jnp.dot(p.astype(BF16), v, preferred_element_type=F32)

    def block(j, step):
        start = pl.multiple_of(j * tk, tk)
        k = k_ref[0, pl.ds(start, tk), :]
        v = v_ref[0, pl.ds(start, tk), :]
        bias = nb_ref[0, 0, pl.ds(j, 1), :]
        for g, qp in zip(groups, q_parts):
            step(g, _qk(qp, k) + bias, v)

    def tile_bias(width):
        parts = [nb_ref[0, 0, pl.ds(i * blocks_per_tile + c, 1), 0:min(tk, width - c * tk)]
                 for c in range(-(-width // tk))]
        return parts[0] if len(parts) == 1 else jnp.concatenate(parts, axis=1)

    start = pl.multiple_of(i * tq, tq)
    for r, (g, qp) in enumerate(zip(groups, q_parts)):
        width = (r + 1) * hs
        row, col = _tile_ids(hs, width)
        a = _qk(qp, k_ref[0, pl.ds(start, width), :]) + tile_bias(width)
        first_update(g, jnp.where(col <= row + r * hs, a, NEG_BIG),
                     v_ref[0, pl.ds(start, width), :])

    def body(state):
        j, _, gap = state
        fixed = gap <= FIXED_MAX_GAP

        @pl.when(fixed)
        def _():
            block(j, update_fixed_max)

        @pl.when(jnp.logical_not(fixed))
        def _():
            block(j, update)

        gap = block_gap(j - 1)
        return j - 1, reachable(j - 1, gap), gap

    last = i * blocks_per_tile - 1
    gap = block_gap(last)
    lax.while_loop(lambda s: s[1] != 0, body, (last, reachable(last, gap), gap))
    if has_prefix:
        @pl.when(logit_gap(pkn_ref[h], 0.0) >= -EXIT_THRESHOLD)
        def _():
            k, v, bias = kp_ref[...], vp_ref[...], pb_ref[pl.ds(h, 1), :]
            for g, qp in zip(groups, q_parts):
                update(g, _qk(qp, k) + bias, v)
    o_ref[0] = acc_ref[...] / l_ref[...]


def _sb_kernel(*refs, tq, tk, single_tile, has_prefix):
    if has_prefix:
        q_ref, k_ref, v_ref, tri_ref, kp_ref, vp_ref, trip_ref, o_ref, run_ref = refs
    else:
        q_ref, k_ref, v_ref, tri_ref, o_ref, run_ref = refs
    i = pl.program_id(2)
    n_split = tq // tk
    first = i * n_split
    groups = [slice(r * tk, (r + 1) * tk) for r in range(n_split)]
    q_parts = [q_ref[0, g, :] for g in groups]

    def update(g, z, v, tri, valid, fresh=False):
        run = jnp.zeros((tk, 1), F32) if fresh else run_ref[g, :]
        chunk = tri.shape[0]
        log_beta = _log_sigmoid(z)
        log_1m = log_beta - z
        if valid is not None:
            log_1m = jnp.where(valid, log_1m, 0.0)
        rests = []
        for c in reversed(range(z.shape[1] // chunk)):
            x = log_1m[:, c * chunk:(c + 1) * chunk]
            hi = x.astype(BF16)
            lo = (x - hi.astype(F32)).astype(BF16)
            rests.append(jnp.dot(hi, tri, preferred_element_type=F32)
                         + jnp.dot(lo, tri, preferred_element_type=F32) + run)
            run = run + jnp.sum(x, axis=-1, keepdims=True)
        rest = rests[0] if len(rests) == 1 else jnp.concatenate(rests[::-1], axis=1)
        a = jnp.exp(log_beta + rest)
        if valid is not None:
            a = jnp.where(valid, a, 0.0)
        pv = jnp.dot(a.astype(BF16), v, preferred_element_type=F32)
        o_ref[0, g, :] = pv if fresh else o_ref[0, g, :] + pv
        run_ref[g, :] = run

    def reachable(g):
        return jnp.max(run_ref[g, :]) >= -EXIT_THRESHOLD

    def own_block(g, qp, j):
        start = pl.multiple_of(j * tk, tk)
        row, col = _tile_ids(tk, tk)
        update(g, _qk(qp, k_ref[0, pl.ds(start, tk), :]), v_ref[0, pl.ds(start, tk), :],
               tri_ref[...], col < row, fresh=True)

    def own_and_previous_block(g, qp, j):
        start = pl.multiple_of((j - 1) * tk, tk)
        row, col = _tile_ids(tk, 2 * tk)
        update(g, _qk(qp, k_ref[0, pl.ds(start, 2 * tk), :]),
               v_ref[0, pl.ds(start, 2 * tk), :], tri_ref[...], col < row + tk, fresh=True)

    def spans(with_previous):
        for r, (g, qp) in enumerate(zip(groups, q_parts)):
            if r > 0 or with_previous:
                own_and_previous_block(g, qp, first + r)
            else:
                own_block(g, qp, first)

    if single_tile:
        spans(False)
    else:
        pl.when(i > 0)(lambda: spans(True))
        pl.when(i == 0)(lambda: spans(False))

    def walk_flags(n):
        return [jnp.logical_and(first + r - 2 - n >= 0, reachable(g))
                for r, g in enumerate(groups)]

    def any_flag(flags):
        return functools.reduce(jnp.logical_or, flags).astype(jnp.int32)

    def body(state):
        n, _ = state
        for r, (g, qp, go) in enumerate(zip(groups, q_parts, walk_flags(n))):
            @pl.when(go)
            def _(g=g, qp=qp, j=first + r - 2 - n):
                start = pl.multiple_of(jnp.maximum(j, 0) * tk, tk)
                update(g, _qk(qp, k_ref[0, pl.ds(start, tk), :]),
                       v_ref[0, pl.ds(start, tk), :], tri_ref[...], None)
        return n + 1, any_flag(walk_flags(n + 1))

    lax.while_loop(lambda s: s[1] != 0, body, (jnp.int32(0), any_flag(walk_flags(0))))
    if has_prefix:
        @pl.when(any_flag([reachable(g) for g in groups]) != 0)
        def _():
            _, col = _tile_ids(tk, META_ROWS)
            for g, qp in zip(groups, q_parts):
                update(g, _qk(qp, kp_ref[...]), vp_ref[...], trip_ref[...], col < N_META)


def _strict_lower(n):
    return jnp.tril(jnp.ones((n, n), BF16), k=-1)


def _attention_specs(batch, seq, t):
    q_spec = pl.BlockSpec((1, t, HEAD_DIM), lambda b, h, i: (b, i, h))
    kv_spec = pl.BlockSpec((1, seq, HEAD_DIM), lambda b, h, i: (b, 0, h))
    o_spec = pl.BlockSpec((1, t, HEAD_DIM), lambda b, h, i: (b, i, h))
    out_shape = jax.ShapeDtypeStruct((batch, seq, D_GROUP), F32)
    params = pltpu.CompilerParams(
        dimension_semantics=("arbitrary", "arbitrary", "arbitrary"),
        vmem_limit_bytes=VMEM_LIMIT)
    return q_spec, kv_spec, o_spec, out_shape, params


def _prefix_spec():
    return pl.BlockSpec((META_ROWS, HEAD_DIM), lambda b, h, i: (0, h))


def _fox_attention(q, k, v, nb, key_norm, prefix, *, tq, tk):
    batch, seq, _ = q.shape
    q_spec, kv_spec, o_spec, out_shape, params = _attention_specs(batch, seq, tq)
    nb4 = nb.reshape(8, batch, seq // tk, tk)
    nb_end = jnp.transpose(nb4[:N_HEADS, :, :, tk - 1], (1, 0, 2))
    smem = pl.BlockSpec(memory_space=pltpu.SMEM)
    in_specs = [q_spec, kv_spec, kv_spec,
                pl.BlockSpec((1, 1, seq // tk, tk), lambda b, h, i: (h, b, 0, 0)), smem, smem]
    args = [q, k, v, nb4, key_norm.reshape(-1), nb_end.reshape(-1)]
    if prefix is not None:
        in_specs += [_prefix_spec(), _prefix_spec(), _resident((8, META_ROWS)), smem]
        args += list(prefix)
    return pl.pallas_call(
        functools.partial(_fox_kernel, tq=tq, tk=tk, has_prefix=prefix is not None),
        grid=(batch, N_HEADS, seq // tq),
        in_specs=in_specs, out_specs=o_spec, out_shape=out_shape,
        scratch_shapes=[pltpu.VMEM((tq, 1), F32), pltpu.VMEM((tq, 1), F32),
                        pltpu.VMEM((tq, HEAD_DIM), F32)],
        compiler_params=params, name="fox_attention",
    )(*args)


def _sb_attention(q, k, v, prefix, *, tq, tk):
    batch, seq, _ = q.shape
    q_spec, kv_spec, o_spec, out_shape, params = _attention_specs(batch, seq, tq)
    in_specs = [q_spec, kv_spec, kv_spec, _resident((tk, tk))]
    args = [q, k, v, _strict_lower(tk)]
    if prefix is not None:
        in_specs += [_prefix_spec(), _prefix_spec(), _resident((META_ROWS, META_ROWS))]
        args += list(prefix) + [_strict_lower(META_ROWS)]
    return pl.pallas_call(
        functools.partial(_sb_kernel, tq=tq, tk=tk, single_tile=seq == tq,
                          has_prefix=prefix is not None),
        grid=(batch, N_HEADS, seq // tq),
        in_specs=in_specs, out_specs=o_spec, out_shape=out_shape,
        scratch_shapes=[pltpu.VMEM((tq, 1), F32)],
        compiler_params=params, name="sb_attention",
    )(*args)


def _out_proj_kernel(of_ref, os_ref, x_ref, g0_ref, b0_ref, gf_ref, gs_ref, wo_ref,
                     g1_ref, b1_ref, h_ref):
    mf = _rms_norm(of_ref[...], gf_ref[...]).astype(BF16)
    ms = _rms_norm(os_ref[...], gs_ref[...]).astype(BF16)
    y = (jnp.dot(mf, wo_ref[0:D_GROUP, :], preferred_element_type=F32)
         + jnp.dot(ms, wo_ref[D_GROUP:, :], preferred_element_type=F32))
    h0 = _layer_norm(x_ref[...], g0_ref[...], b0_ref[...])
    h_ref[...] = _layer_norm(DEEPNORM_ALPHA * h0 + y, g1_ref[...], b1_ref[...])


def _out_proj(o_f, o_s, x2d, ln0_g, ln0_b, g_fox, g_sb, w_o, ln1_g, ln1_b, *, tm):
    rows, d = x2d.shape
    row_spec = lambda width: pl.BlockSpec((tm, width), lambda i: (i, 0))
    return pl.pallas_call(
        _out_proj_kernel,
        grid=(rows // tm,),
        in_specs=[row_spec(D_GROUP), row_spec(D_GROUP), row_spec(d),
                  _resident((1, d)), _resident((1, d)),
                  _resident((1, D_GROUP)), _resident((1, D_GROUP)),
                  _resident(w_o.shape), _resident((1, d)), _resident((1, d))],
        out_specs=row_spec(d),
        out_shape=jax.ShapeDtypeStruct((rows, d), F32),
        compiler_params=pltpu.CompilerParams(
            dimension_semantics=("arbitrary",), vmem_limit_bytes=VMEM_LIMIT),
        name="out_proj",
    )(o_f, o_s, x2d, ln0_g, ln0_b, g_fox, g_sb, w_o, ln1_g, ln1_b)


def _gate_kernel(h_ref, w_ref, g_ref):
    g_ref[...] = jnp.dot(h_ref[...].astype(BF16), w_ref[...], preferred_element_type=F32)


def _gate_rows(h_rows, w_gu, d_ff, *, tf):
    rows, d = h_rows.shape
    return pl.pallas_call(
        _gate_kernel,
        grid=(d_ff // tf,),
        in_specs=[_resident((rows, d)), pl.BlockSpec((d, tf), lambda f: (0, f))],
        out_specs=pl.BlockSpec((rows, tf), lambda f: (0, f)),
        out_shape=jax.ShapeDtypeStruct((rows, d_ff), F32),
        compiler_params=pltpu.CompilerParams(dimension_semantics=("arbitrary",)),
        name="gate_rows",
    )(h_rows, w_gu)


def _ffn_kernel(h_ref, wg_ref, wu_ref, wd_ref, cw_ref, cb_ref, gi_ref, g2_ref, b2_ref,
                o_ref, hb_ref, hist_ref, *, tm):
    i = pl.program_id(1)
    f = pl.program_id(2)

    @pl.when(f == 0)
    def _():
        h = h_ref[0]
        hb_ref[...] = h.astype(BF16)
        o_ref[0] = DEEPNORM_ALPHA * h

    @pl.when(i == 0)
    def _():
        hist_ref[f] = gi_ref[...]

    hb = hb_ref[...]
    g = jnp.dot(hb, wg_ref[...], preferred_element_type=F32)
    u = jnp.dot(hb, wu_ref[...], preferred_element_type=F32)
    hist = hist_ref[f]
    prev1 = hist[7:8, :]
    prev2 = hist[6:7, :]
    hist_ref[f] = g[tm - 8:, :]
    row = lax.broadcasted_iota(jnp.int32, g.shape, 0)
    g1 = jnp.where(row == 0, prev1, pltpu.roll(g, 1, 0))
    g2 = jnp.where(row == 0, prev2, jnp.where(row == 1, prev1, pltpu.roll(g, 2, 0)))
    cw = cw_ref[...]
    gc = cw[0:1, :] * g2 + cw[1:2, :] * g1 + cw[2:3, :] * g + cb_ref[...]
    act = gc * (1.0 / (1.0 + jnp.exp(-gc))) * u
    o_ref[0] += jnp.dot(act.astype(BF16), wd_ref[...], preferred_element_type=F32)

    @pl.when(f == pl.num_programs(2) - 1)
    def _():
        o_ref[0] = _layer_norm(o_ref[0], g2_ref[...], b2_ref[...])


def _ffn(h1, w_gu, w_down, conv_w, conv_b, g_init, ln2_g, ln2_b, *, tm, tf):
    batch, seq, d = h1.shape
    d_ff = w_down.shape[0]
    n_f = d_ff // tf
    return pl.pallas_call(
        functools.partial(_ffn_kernel, tm=tm),
        grid=(batch, seq // tm, n_f),
        in_specs=[
            pl.BlockSpec((1, tm, d), lambda b, i, f: (b, i, 0)),
            pl.BlockSpec((d, tf), lambda b, i, f: (0, f)),
            pl.BlockSpec((d, tf), lambda b, i, f: (0, n_f + f)),
            pl.BlockSpec((tf, d), lambda b, i, f: (f, 0)),
            pl.BlockSpec((CONV_WIDTH, tf), lambda b, i, f: (0, f)),
            pl.BlockSpec((1, tf), lambda b, i, f: (0, f)),
            pl.BlockSpec((8, tf), lambda b, i, f: (0, f)),
            _resident((1, d)), _resident((1, d)),
        ],
        out_specs=pl.BlockSpec((1, tm, d), lambda b, i, f: (b, i, 0)),
        out_shape=jax.ShapeDtypeStruct((batch, seq, d), F32),
        scratch_shapes=[pltpu.VMEM((tm, d), BF16), pltpu.VMEM((n_f, 8, tf), F32)],
        compiler_params=pltpu.CompilerParams(
            dimension_semantics=("arbitrary", "arbitrary", "arbitrary"),
            vmem_limit_bytes=VMEM_LIMIT),
        name="conv_ffn",
    )(h1, w_gu, w_gu, w_down, conv_w, conv_b, g_init, ln2_g, ln2_b)


def _pick(n, preferred):
    return preferred if n % preferred == 0 else n


def kernel(x, meta, ln0_g, ln0_b, w_in, b_f, g_fox, g_sb, w_o, ln1_g, ln1_b, w_gu, conv_w,
           conv_b, w_down, ln2_g, ln2_b):
    batch, seq, d = x.shape
    assert w_in.shape[0] == 1, "single layer"
    assert d == 2 * D_GROUP
    rows = batch * seq
    d_ff = w_down.shape[1]
    tm = _pick(seq, 512)
    tq_fox = _pick(seq, 1024)
    tq_sb = _pick(seq, 2048)
    tk_sb = _pick(seq, 256)
    tm_ffn = _pick(seq, 512)
    tf = _pick(d_ff, 512)

    row = lambda a: a.reshape(1, -1).astype(F32)
    w_fox, w_sb, w_gate = _split_w_in(jnp.transpose(w_in[0].astype(F32)), tc=256)
    bf_row = jnp.pad(b_f[0].astype(F32), (0, GATE_COLS - N_HEADS)).reshape(1, GATE_COLS)
    w_o_b = w_o[0].astype(BF16)
    w_gu_b = w_gu[0].astype(BF16)
    w_down_b = w_down[0].astype(BF16)
    ln0 = (row(ln0_g), row(ln0_b))
    ln1 = (row(ln1_g[0]), row(ln1_b[0]))
    ln2 = (row(ln2_g[0]), row(ln2_b[0]))
    gf, gs = row(g_fox[0]), row(g_sb[0])

    def mixer(x2d, seqs, seq_len, t, tq, tq_sb, tk, meta_keys):
        qf, kf, vf, qs, ks, vs, nb, kn = _in_proj(x2d, *ln0, w_fox, w_sb, w_gate, bf_row,
                                                  rows_per_seq=seq_len, tm=t)
        kn = jnp.sqrt(kn[:, :N_HEADS, 0]).reshape(seqs, seq_len // t, N_HEADS)
        kn = lax.cummax(jnp.transpose(kn, (0, 2, 1)), axis=2)
        prefix_f = prefix_s = None
        if meta_keys is not None:
            kf_m, vf_m, nb_m, kn_m, ks_m, vs_m = meta_keys
            kn = jnp.maximum(kn, kn_m[0, :, :1])
            valid = jnp.arange(META_ROWS)[None, :] < N_META
            pbias = jnp.where(valid, nb_m - nb_m[:, N_META - 1:N_META], NEG_BIG)
            prefix_f = (kf_m, vf_m, pbias, kn_m.reshape(-1))
            prefix_s = (ks_m, vs_m)
        shape3 = (seqs, seq_len, D_GROUP)
        o_f = _fox_attention(qf.reshape(shape3), kf.reshape(shape3), vf.reshape(shape3),
                             nb, kn, prefix_f, tq=tq, tk=t)
        o_s = _sb_attention(qs.reshape(shape3), ks.reshape(shape3), vs.reshape(shape3),
                            prefix_s, tq=tq_sb, tk=tk)
        h1 = _out_proj(o_f.reshape(-1, D_GROUP), o_s.reshape(-1, D_GROUP), x2d, *ln0,
                       gf, gs, w_o_b, *ln1, tm=t)
        return h1, (kf, vf, nb, kn, ks, vs)

    meta_pad = jnp.zeros((META_ROWS, d), F32).at[:N_META].set(meta.astype(F32))
    h1_meta, meta_keys = mixer(meta_pad, 1, META_ROWS, META_ROWS, META_ROWS, META_ROWS,
                               META_ROWS, None)
    g_meta = _gate_rows(h1_meta[:N_META], w_gu_b, d_ff, tf=tf)
    g_init = g_meta[N_META - 8:N_META]

    h1, _ = mixer(x.reshape(rows, d).astype(F32), batch, seq, tm, tq_fox, tq_sb, tk_sb,
                  meta_keys)
    out = _ffn(h1.reshape(batch, seq, d), w_gu_b, w_down_b, conv_w[0].astype(F32),
               row(conv_b[0]), g_init, *ln2, tm=tm_ffn, tf=tf)
    return out.astype(x.dtype)
```
